```python
import jax, jax.numpy as jnp
from jax import lax
import numpy as np

D_MODEL = 4096
BATCH = 1
SEQ = 16384
DEPTH = 1
DEC_BATCH = 4
DEC_SEQ = 2048
PAST_LEN = 128

GRID_W = 64
MLSTM_WIDTH = D_MODEL // 2
MLSTM_HEADS = 8
MLSTM_HEAD_DIM = MLSTM_WIDTH // MLSTM_HEADS
MLSTM_CHUNK = 64
NA_WIDTH = D_MODEL - MLSTM_WIDTH
NA_HEADS = 16
NA_HEAD_DIM = NA_WIDTH // NA_HEADS
NA_WIN_ROWS = 8
NA_WIN_COLS = 16
N_GATES = 4 * MLSTM_HEADS
PROJ_WIDTH = 4 * MLSTM_WIDTH + N_GATES + 3 * NA_WIDTH
PEER_HEADS = 8
PEER_QUERY_DIM = 256
PEER_HALF = PEER_QUERY_DIM // 2
PEER_N_KEYS = 128
PEER_N_EXPERTS = PEER_N_KEYS * PEER_N_KEYS
PEER_TOPK = 16
PEER_TOKEN_BLOCK = 64
EPS = 1e-6

kernel_name = 'hymba_mlstm_natten_peer_encoder'


def _rmsnorm(x, g):
    xf = x.astype(jnp.float32)
    y = xf * lax.rsqrt(jnp.mean(xf * xf, axis=-1, keepdims=True) + EPS)
    return (y * g.astype(jnp.float32)).astype(x.dtype)


def _mlstm_direction(q, k, v, logi, logf):
    B, H, L, dh = q.shape
    nc = L // MLSTM_CHUNK

    def chunks(a):
        return jnp.moveaxis(a.reshape((B, H, nc, MLSTM_CHUNK) + a.shape[3:]), 2, 0)

    tril = jnp.tril(jnp.ones((MLSTM_CHUNK, MLSTM_CHUNK), dtype=bool))

    def step(carry, inp):
        C, n, m = carry
        qc, kc, vc, ic, fc = inp
        b = jnp.cumsum(fc, axis=-1)
        d = b[..., :, None] - b[..., None, :] + ic[..., None, :]
        d = jnp.where(tril, d, -jnp.inf)
        inter = b + m[..., None]
        m_t = jnp.maximum(inter, jnp.max(d, axis=-1))
        w_inter = jnp.exp(inter - m_t)
        p = jnp.exp(d - m_t[..., None]) * jnp.einsum('bhtd,bhsd->bhts', qc, kc)
        num = w_inter[..., None] * jnp.einsum('bhtd,bhde->bhte', qc, C) + jnp.einsum('bhts,bhse->bhte', p, vc)
        den = w_inter * jnp.einsum('bhtd,bhd->bht', qc, n) + jnp.sum(p, axis=-1)
        h = num / jnp.maximum(jnp.abs(den), jnp.exp(-m_t))[..., None]
        b_last = b[..., -1]
        dec = b_last[..., None] - b + ic
        m_new = jnp.maximum(b_last + m, jnp.max(dec, axis=-1))
        w_old = jnp.exp(b_last + m - m_new)
        w_s = jnp.exp(dec - m_new[..., None])
        C = w_old[..., None, None] * C + jnp.einsum('bhs,bhsd,bhse->bhde', w_s, kc, vc)
        n = w_old[..., None] * n + jnp.einsum('bhs,bhsd->bhd', w_s, kc)
        return (C, n, m_new), h

    init = (jnp.zeros((B, H, dh, dh), jnp.float32),
            jnp.zeros((B, H, dh), jnp.float32),
            jnp.zeros((B, H), jnp.float32))
    _, hs = lax.scan(step, init, (chunks(q), chunks(k), chunks(v), chunks(logi), chunks(logf)))
    return jnp.moveaxis(hs, 0, 2).reshape(B, H, L, dh)


def _neighbourhood_attention(q, k, v, rpb):
    B, L, H, dh = q.shape
    rows = L // GRID_W
    wr = min(NA_WIN_ROWS, rows)
    qg = q.reshape(B, rows, GRID_W, H, dh)
    kg = k.reshape(B, rows, GRID_W, H, dh)
    vg = v.reshape(B, rows, GRID_W, H, dh)
    r = jnp.arange(rows)
    rs = jnp.clip(r - wr // 2, 0, rows - wr)
    key_rows = rs[:, None] + jnp.arange(wr)[None, :]
    kb = kg[:, key_rows]
    vb = vg[:, key_rows]
    s = jnp.einsum('brqhd,brjkhd->bhrqjk', qg, kb).astype(jnp.float32) * (dh ** -0.5)
    c = jnp.arange(GRID_W)
    cs = jnp.clip(c - NA_WIN_COLS // 2, 0, GRID_W - NA_WIN_COLS)
    col_ok = (c[None, :] >= cs[:, None]) & (c[None, :] < cs[:, None] + NA_WIN_COLS)
    dr = key_rows - r[:, None] + (NA_WIN_ROWS - 1)
    dc = jnp.clip(c[None, :] - c[:, None], -(NA_WIN_COLS - 1), NA_WIN_COLS - 1) + (NA_WIN_COLS - 1)
    bias = rpb.astype(jnp.float32)[:, dr[:, None, :, None], dc[None, :, None, :]]
    s = jnp.where(col_ok[None, None, None, :, None, :], s + bias[None], -jnp.inf)
    p = jax.nn.softmax(s.reshape(B, H, rows, GRID_W, wr * GRID_W), axis=-1)
    p = p.reshape(B, H, rows, GRID_W, wr, GRID_W).astype(v.dtype)
    o = jnp.einsum('bhrqjk,brjkhd->brqhd', p, vb)
    return o.reshape(B, L, H * dh)


def _mixer(h, w_in, gate_b, mlstm_norm_g, na_rpb, w_out):
    B, L, _ = h.shape
    proj = h @ w_in
    mw, aw = MLSTM_WIDTH, NA_WIDTH
    splits = [mw, 2 * mw, 3 * mw, 4 * mw, 4 * mw + N_GATES,
              4 * mw + N_GATES + aw, 4 * mw + N_GATES + 2 * aw]
    mq, mk, mv, mo, gates, nq, nk, nv = jnp.split(proj, splits, axis=-1)

    def heads(a):
        return a.reshape(B, L, MLSTM_HEADS, MLSTM_HEAD_DIM).transpose(0, 2, 1, 3).astype(jnp.float32)

    q = heads(mq)
    k = heads(mk) * (MLSTM_HEAD_DIM ** -0.5)
    v = heads(mv)
    g = (gates.astype(jnp.float32) + gate_b.astype(jnp.float32)).reshape(B, L, 4, MLSTM_HEADS)
    g = g.transpose(2, 0, 3, 1)
    h_fwd = _mlstm_direction(q, k, v, g[0], jax.nn.log_sigmoid(g[1]))
    fl = lambda a: jnp.flip(a, axis=2)
    h_bwd = fl(_mlstm_direction(fl(q), fl(k), fl(v), fl(g[2]), fl(jax.nn.log_sigmoid(g[3]))))
    hm = h_fwd + h_bwd
    hm = hm * lax.rsqrt(jnp.mean(hm * hm, axis=-1, keepdims=True) + EPS)
    hm = hm.transpose(0, 2, 1, 3).reshape(B, L, MLSTM_WIDTH) * mlstm_norm_g.astype(jnp.float32)
    hm = (hm * jax.nn.sigmoid(mo.astype(jnp.float32))).astype(h.dtype)

    hn = _neighbourhood_attention(nq.reshape(B, L, NA_HEADS, NA_HEAD_DIM),
                                  nk.reshape(B, L, NA_HEADS, NA_HEAD_DIM),
                                  nv.reshape(B, L, NA_HEADS, NA_HEAD_DIM), na_rpb)
    return jnp.concatenate([hm, hn], axis=-1) @ w_out


def _peer(h, w_q, sub_k1, sub_k2, u, v):
    B, L, D = h.shape
    q = (h @ w_q).astype(jnp.float32).reshape(B, L, PEER_HEADS, 2, PEER_HALF)
    s1 = jnp.einsum('blhd,hnd->blhn', q[..., 0, :], sub_k1.astype(jnp.float32))
    s2 = jnp.einsum('blhd,hnd->blhn', q[..., 1, :], sub_k2.astype(jnp.float32))
    v1, i1 = lax.top_k(s1, PEER_TOPK)
    v2, i2 = lax.top_k(s2, PEER_TOPK)
    cand = (v1[..., :, None] + v2[..., None, :]).reshape(B, L, PEER_HEADS, PEER_TOPK * PEER_TOPK)
    sc, ci = lax.top_k(cand, PEER_TOPK)
    e1 = jnp.take_along_axis(i1, ci // PEER_TOPK, axis=-1)
    e2 = jnp.take_along_axis(i2, ci % PEER_TOPK, axis=-1)
    expert = e1 * PEER_N_KEYS + e2
    gw = jax.nn.softmax(sc, axis=-1)
    nblk = (B * L) // PEER_TOKEN_BLOCK
    hb = h.reshape(nblk, PEER_TOKEN_BLOCK, D)
    eb = expert.reshape(nblk, PEER_TOKEN_BLOCK, PEER_HEADS * PEER_TOPK)
    gb = gw.astype(h.dtype).reshape(nblk, PEER_TOKEN_BLOCK, PEER_HEADS * PEER_TOPK)

    def block(args):
        xt, et, gt = args
        act = jax.nn.gelu(jnp.einsum('td,tkd->tk', xt, u[et]))
        return jnp.einsum('tk,tkd->td', gt * act, v[et])

    out = lax.map(block, (hb, eb, gb))
    return out.reshape(B, L, D)


def _layer(x, c, ada_w, ada_b, norm1_g, w_in, gate_b, mlstm_norm_g, na_rpb, w_out,
           norm2_g, peer_wq, peer_k1, peer_k2, peer_u, peer_v):
    mod = jax.nn.silu(c) @ ada_w + ada_b
    sh1, sc1, g1, sh2, sc2, g2 = jnp.split(mod[:, None, :], 6, axis=-1)
    h = _rmsnorm(x, norm1_g) * (1 + sc1) + sh1
    x = x + g1 * _mixer(h, w_in, gate_b, mlstm_norm_g, na_rpb, w_out)
    h = _rmsnorm(x, norm2_g) * (1 + sc2) + sh2
    x = x + g2 * _peer(h, peer_wq, peer_k1, peer_k2, peer_u, peer_v)
    return x


def _trunk(x, c, ada_w, ada_b, norm1_g, w_in, gate_b, mlstm_norm_g, na_rpb, w_out,
           norm2_g, peer_wq, peer_k1, peer_k2, peer_u, peer_v, final_g):
    for layer in range(DEPTH):
        x = _layer(x, c, ada_w[layer], ada_b[layer], norm1_g[layer], w_in[layer], gate_b[layer],
                   mlstm_norm_g[layer], na_rpb[layer], w_out[layer], norm2_g[layer],
                   peer_wq[layer], peer_k1[layer], peer_k2[layer], peer_u[layer], peer_v[layer])
    return _rmsnorm(x, final_g)


def setup_inputs(seed: int = 0) -> dict:
    key = jax.random.key(seed)
    ks = jax.random.split(key, 24)
    D = D_MODEL
    nrm = jax.random.normal
    x_prompt = nrm(ks[0], (BATCH, SEQ, D), jnp.float32)
    x_sample = nrm(ks[1], (DEC_BATCH, DEC_SEQ, D), jnp.float32)
    c_prompt = nrm(ks[2], (BATCH, D), jnp.float32)
    c_sample = nrm(ks[3], (DEC_BATCH, D), jnp.float32)
    ada_w = nrm(ks[4], (DEPTH, D, 6 * D), jnp.float32) * (0.5 * D ** -0.5)
    ada_b = nrm(ks[5], (DEPTH, 6 * D), jnp.float32) * 0.02
    norm1_g = 1.0 + 0.01 * nrm(ks[6], (DEPTH, D), jnp.float32)
    w_in = nrm(ks[7], (DEPTH, D, PROJ_WIDTH), jnp.float32) * (D ** -0.5)
    i_bias = 0.1 * nrm(ks[8], (DEPTH, 2, 1, MLSTM_HEADS), jnp.float32)
    f_bias = jnp.linspace(3.0, 6.0, MLSTM_HEADS, dtype=jnp.float32) + 0.1 * nrm(ks[9], (DEPTH, 2, 1, MLSTM_HEADS), jnp.float32)
    gate_b = jnp.concatenate([i_bias, f_bias], axis=2).reshape(DEPTH, N_GATES)
    mlstm_norm_g = 1.0 + 0.01 * nrm(ks[10], (DEPTH, MLSTM_WIDTH), jnp.float32)
    na_rpb = 0.2 * nrm(ks[11], (DEPTH, NA_HEADS, 2 * NA_WIN_ROWS - 1, 2 * NA_WIN_COLS - 1), jnp.float32)
    w_out = nrm(ks[12], (DEPTH, D, D), jnp.float32) * (D ** -0.5)
    norm2_g = 1.0 + 0.01 * nrm(ks[13], (DEPTH, D), jnp.float32)
    peer_wq = nrm(ks[14], (DEPTH, D, PEER_HEADS * PEER_QUERY_DIM), jnp.float32) * (D ** -0.5)
    peer_k1 = nrm(ks[15], (DEPTH, PEER_HEADS, PEER_N_KEYS, PEER_HALF), jnp.float32) * (PEER_HALF ** -0.5)
    peer_k2 = nrm(ks[16], (DEPTH, PEER_HEADS, PEER_N_KEYS, PEER_HALF), jnp.float32) * (PEER_HALF ** -0.5)
    peer_u = nrm(ks[17], (DEPTH, PEER_N_EXPERTS, D), jnp.float32) * (D ** -0.5)
    peer_v = nrm(ks[18], (DEPTH, PEER_N_EXPERTS, D), jnp.float32) * 0.5
    final_g = 1.0 + 0.01 * nrm(ks[19], (D,), jnp.float32)
    return {'x_prompt': x_prompt, 'x_sample': x_sample, 'c_prompt': c_prompt, 'c_sample': c_sample,
            'ada_w': ada_w, 'ada_b': ada_b, 'norm1_g': norm1_g, 'w_in': w_in, 'gate_b': gate_b,
            'mlstm_norm_g': mlstm_norm_g, 'na_rpb': na_rpb, 'w_out': w_out, 'norm2_g': norm2_g,
            'peer_wq': peer_wq, 'peer_k1': peer_k1, 'peer_k2': peer_k2, 'peer_u': peer_u,
            'peer_v': peer_v, 'final_g': final_g}


def reference(x_prompt, x_sample, c_prompt, c_sample, ada_w, ada_b, norm1_g, w_in, gate_b,
              mlstm_norm_g, na_rpb, w_out, norm2_g, peer_wq, peer_k1, peer_k2, peer_u, peer_v,
              final_g):
    y_prompt = _trunk(x_prompt, c_prompt, ada_w, ada_b, norm1_g, w_in, gate_b, mlstm_norm_g,
                      na_rpb, w_out, norm2_g, peer_wq, peer_k1, peer_k2, peer_u, peer_v, final_g)
    y_sample = _trunk(x_sample, c_sample, ada_w, ada_b, norm1_g, w_in, gate_b, mlstm_norm_g,
                      na_rpb, w_out, norm2_g, peer_wq, peer_k1, peer_k2, peer_u, peer_v, final_g)
    return (y_prompt, y_sample)
```

```python
import functools

import jax
import jax.numpy as jnp
from jax import lax
from jax.experimental import pallas as pl
from jax.experimental.pallas import tpu as pltpu

F32 = jnp.float32
BF16 = jnp.bfloat16

GRID_W = 64
MLSTM_HEADS = 8
MLSTM_HEAD_DIM = 256
NA_HEADS = 16
NA_HEAD_DIM = 128
NA_WIN_ROWS = 8
NA_WIN_COLS = 16
PEER_HEADS = 8
PEER_HALF = 128
PEER_N_KEYS = 128
PEER_TOPK = 16
EPS = 1e-6

LANES = 128
VMEM_LIMIT = 56 * 1024 * 1024

NEG_INF = float("-inf")


def _cparams(sem):
    return pltpu.CompilerParams(dimension_semantics=sem, vmem_limit_bytes=VMEM_LIMIT)


def _dot(a, b):
    return jnp.dot(a, b, preferred_element_type=F32)


def _dot_nt(a, b):
    return lax.dot_general(a, b, (((1,), (1,)), ((), ())), preferred_element_type=F32)


def _mod_kernel(c_ref, w_ref, b_ref, o_ref):
    c = c_ref[...]
    s = c * jax.nn.sigmoid(c)
    hi = s.astype(BF16)
    lo = (s - hi.astype(F32)).astype(BF16)
    w = w_ref[...].astype(BF16)
    o_ref[...] = _dot(hi, w) + _dot(lo, w) + b_ref[...]


def _modulation(c, w, b, tn=512):
    r, d = c.shape
    n = w.shape[1]
    return pl.pallas_call(
        _mod_kernel,
        grid=(n // tn,),
        in_specs=[pl.BlockSpec((r, d), lambda j: (0, 0)),
                  pl.BlockSpec((d, tn), lambda j: (0, j)),
                  pl.BlockSpec((1, tn), lambda j: (0, j))],
        out_specs=pl.BlockSpec((r, tn), lambda j: (0, j)),
        out_shape=jax.ShapeDtypeStruct((r, n), F32),
        compiler_params=_cparams(("parallel",)),
        name="modulation",
    )(c, w, b)


def _ada_norm(x, g, sc, sh):
    ms = jnp.mean(x * x, axis=-1, keepdims=True)
    y = x * lax.rsqrt(ms + EPS)
    return (y * g) * (1.0 + sc) + sh


def _inproj_kernel(x_ref, g_ref, sc_ref, sh_ref, w_ref, wg_ref, o_ref, og_ref, h_scr):
    @pl.when(pl.program_id(2) == 0)
    def _():
        h = _ada_norm(x_ref[0], g_ref[...], sc_ref[0], sh_ref[0]).astype(BF16)
        h_scr[...] = h
        og_ref[0] = _dot(h, wg_ref[...])

    o_ref[0] = _dot(h_scr[...], w_ref[...]).astype(BF16)


def _inproj(x, g, sc, sh, w, wg, tm=512, tn=1024):
    bsz, l, d = x.shape
    n = w.shape[1]
    ng = wg.shape[1]
    return pl.pallas_call(
        _inproj_kernel,
        grid=(bsz, l // tm, n // tn),
        in_specs=[pl.BlockSpec((1, tm, d), lambda b, i, j: (b, i, 0)),
                  pl.BlockSpec((1, d), lambda b, i, j: (0, 0)),
                  pl.BlockSpec((1, 1, d), lambda b, i, j: (b, 0, 0)),
                  pl.BlockSpec((1, 1, d), lambda b, i, j: (b, 0, 0)),
                  pl.BlockSpec((d, tn), lambda b, i, j: (0, j)),
                  pl.BlockSpec((d, ng), lambda b, i, j: (0, 0))],
        out_specs=[pl.BlockSpec((1, tm, tn), lambda b, i, j: (b, i, j)),
                   pl.BlockSpec((1, tm, ng), lambda b, i, j: (b, i, 0))],
        out_shape=[jax.ShapeDtypeStruct((bsz, l, n), BF16),
                   jax.ShapeDtypeStruct((bsz, l, ng), F32)],
        scratch_shapes=[pltpu.VMEM((tm, d), BF16)],
        compiler_params=_cparams(("parallel", "parallel", "arbitrary")),
        name="inproj",
    )(x, g, sc, sh, w, wg)


def _log_sigmoid(x):
    return jnp.minimum(x, 0.0) - jnp.log(1.0 + jnp.exp(-jnp.abs(x)))


def _mlstm_chunk(q, k, v, i_col, f_col, i_row, f_row, c_st, n_st, m_st, reverse):
    t = q.shape[0]
    kscale = MLSTM_HEAD_DIM ** -0.5
    r_i = lax.broadcasted_iota(jnp.int32, (t, t), 0)
    c_i = lax.broadcasted_iota(jnp.int32, (t, t), 1)
    seen = (c_i >= r_i) if reverse else (c_i <= r_i)
    seen_t = (r_i >= c_i) if reverse else (r_i <= c_i)
    b_col = jnp.sum(jnp.where(seen, f_row, 0.0), axis=1, keepdims=True)
    b_row = jnp.sum(jnp.where(seen_t, f_col, 0.0), axis=0, keepdims=True)
    d = jnp.where(seen, b_col - b_row + i_row, NEG_INF)
    inter = b_col + m_st
    m_t = jnp.maximum(inter, jnp.max(d, axis=1, keepdims=True))
    w_inter = jnp.exp(inter - m_t)
    p = jnp.exp(d - m_t) * (_dot_nt(q, k) * kscale)
    num = w_inter * _dot(q, c_st.astype(BF16)) + _dot(p.astype(BF16), v)
    den = (w_inter * jnp.sum(q.astype(F32) * n_st, axis=1, keepdims=True)
           + jnp.sum(p, axis=1, keepdims=True))
    h = num / jnp.maximum(jnp.abs(den), jnp.exp(-m_t))
    b_last = b_col[0:1] if reverse else b_col[t - 1:t]
    dec = b_last - b_col + i_col
    m_new = jnp.maximum(b_last + m_st, jnp.max(dec, axis=0, keepdims=True))
    w_old = jnp.exp(b_last + m_st - m_new)
    w_s = jnp.exp(dec - m_new)
    kw = k.astype(F32) * (w_s * kscale)
    c_new = w_old * c_st + _dot(kw.T.astype(BF16), v)
    n_new = w_old * n_st + jnp.sum(kw, axis=0, keepdims=True)
    return h, c_new, n_new, m_new


def _mlstm_kernel(*refs, reverse, hb):
    if reverse:
        q_ref, k_ref, v_ref, gc_ref, gr_ref, bc_ref, br_ref, o_ref, c_scr, n_scr, m_scr = refs
    else:
        (q_ref, k_ref, v_ref, gc_ref, gr_ref, bc_ref, br_ref, hb_ref, mo_ref, ng_ref,
         o_ref, c_scr, n_scr, m_scr) = refs

    @pl.when(pl.program_id(2) == 0)
    def _():
        c_scr[...] = jnp.zeros_like(c_scr)
        n_scr[...] = jnp.zeros_like(n_scr)
        m_scr[...] = jnp.zeros_like(m_scr)

    dh = MLSTM_HEAD_DIM
    gi = 2 if reverse else 0
    for hh in range(hb):
        cols = slice(hh * dh, (hh + 1) * dh)
        gc = gc_ref[0, hh] + bc_ref[hh]
        gr = gr_ref[0, hh] + br_ref[hh]
        h, c_new, n_new, m_new = _mlstm_chunk(
            q_ref[0, :, cols], k_ref[0, :, cols], v_ref[0, :, cols],
            gc[:, gi:gi + 1], _log_sigmoid(gc[:, gi + 1:gi + 2]),
            gr[gi:gi + 1, :], _log_sigmoid(gr[gi + 1:gi + 2, :]),
            c_scr[hh], n_scr[hh], m_scr[hh], reverse)
        c_scr[hh] = c_new
        n_scr[hh] = n_new
        m_scr[hh] = m_new
        if reverse:
            o_ref[0, :, cols] = h
        else:
            ht = h + hb_ref[0, :, cols]
            ht = ht * lax.rsqrt(jnp.mean(ht * ht, axis=-1, keepdims=True) + EPS)
            ht = ht * ng_ref[:, cols]
            o_ref[0, :, cols] = (ht * jax.nn.sigmoid(mo_ref[0, :, cols].astype(F32))).astype(BF16)


def _mlstm_direction(pm, gc, gr, bc, br, reverse, h_bwd=None, norm_g=None, t=256, hb=2):
    bsz, l, _ = pm.shape
    nh = MLSTM_HEADS
    dh = MLSTM_HEAD_DIM
    w = nh * dh
    nc = l // t
    ng = nh // hb
    wb = hb * dh

    def ci(c):
        return (nc - 1 - c) if reverse else c

    in_specs = [pl.BlockSpec((1, t, wb), lambda b, g, c: (b, ci(c), g)),
                pl.BlockSpec((1, t, wb), lambda b, g, c: (b, ci(c), ng + g)),
                pl.BlockSpec((1, t, wb), lambda b, g, c: (b, ci(c), 2 * ng + g)),
                pl.BlockSpec((1, hb, t, 4), lambda b, g, c: (b, g, ci(c), 0)),
                pl.BlockSpec((1, hb, 4, t), lambda b, g, c: (b, g, 0, ci(c))),
                pl.BlockSpec((hb, 1, 4), lambda b, g, c: (g, 0, 0)),
                pl.BlockSpec((hb, 4, 1), lambda b, g, c: (g, 0, 0))]
    args = [pm, pm, pm, gc, gr, bc, br]
    if not reverse:
        in_specs += [pl.BlockSpec((1, t, wb), lambda b, g, c: (b, c, g)),
                     pl.BlockSpec((1, t, wb), lambda b, g, c: (b, c, 3 * ng + g)),
                     pl.BlockSpec((1, wb), lambda b, g, c: (0, g))]
        args += [h_bwd, pm, norm_g]
    return pl.pallas_call(
        functools.partial(_mlstm_kernel, reverse=reverse, hb=hb),
        grid=(bsz, ng, nc),
        in_specs=in_specs,
        out_specs=pl.BlockSpec((1, t, wb), lambda b, g, c: (b, ci(c), g)),
        out_shape=jax.ShapeDtypeStruct((bsz, l, w), F32 if reverse else BF16),
        scratch_shapes=[pltpu.VMEM((hb, dh, dh), F32),
                        pltpu.VMEM((hb, 1, dh), F32),
                        pltpu.VMEM((hb, 1, 1), F32)],
        compiler_params=_cparams(("parallel", "parallel", "arbitrary")),
        name="mlstm_bwd" if reverse else "mlstm_fwd",
    )(*args)


def _na_bias_table(rpb):
    c = jnp.arange(GRID_W)
    cs = jnp.clip(c - NA_WIN_COLS // 2, 0, GRID_W - NA_WIN_COLS)
    col_ok = (c[None, :] >= cs[:, None]) & (c[None, :] < cs[:, None] + NA_WIN_COLS)
    dc = jnp.clip(c[None, :] - c[:, None], -(NA_WIN_COLS - 1), NA_WIN_COLS - 1) + (NA_WIN_COLS - 1)
    dr = jnp.arange(NA_WIN_ROWS)[:, None] + jnp.arange(NA_WIN_ROWS)[None, :]
    tab = rpb.astype(F32)[:, dr[:, None, :, None], dc[None, :, None, :]]
    tab = jnp.where(col_ok[None, None, :, None, :], tab, NEG_INF)
    return tab.reshape(rpb.shape[0], NA_WIN_ROWS, GRID_W, NA_WIN_ROWS * GRID_W)


def _na_kernel(q_ref, k_ref, v_ref, bias_ref, o_ref, *, rows, rblk):
    win = NA_WIN_ROWS * GRID_W
    scale = NA_HEAD_DIM ** -0.5
    for jr in range(rblk):
        r = pl.program_id(2) * rblk + jr
        start = jnp.clip(r - NA_WIN_ROWS // 2, 0, rows - NA_WIN_ROWS)
        off = start - r + (NA_WIN_ROWS - 1)
        kpos = pl.multiple_of(start * GRID_W, GRID_W)
        kw = k_ref[0, pl.ds(kpos, win), :]
        vw = v_ref[0, pl.ds(kpos, win), :]
        q = q_ref[0, jr * GRID_W:(jr + 1) * GRID_W, :]
        s = _dot_nt(q, kw) * scale + bias_ref[0, off]
        p = jnp.exp(s - jnp.max(s, axis=-1, keepdims=True))
        o = _dot(p.astype(BF16), vw) / jnp.sum(p, axis=-1, keepdims=True)
        o_ref[0, jr * GRID_W:(jr + 1) * GRID_W, :] = o.astype(BF16)


def _neighbourhood_attention(pn, bias_tab, col0, rblk=8):
    bsz, l, _ = pn.shape
    nh, dh = NA_HEADS, NA_HEAD_DIM
    rows = l // GRID_W
    assert rows >= NA_WIN_ROWS and rows % rblk == 0
    tq = rblk * GRID_W
    return pl.pallas_call(
        functools.partial(_na_kernel, rows=rows, rblk=rblk),
        grid=(bsz, nh, rows // rblk),
        in_specs=[pl.BlockSpec((1, tq, dh), lambda b, h, i: (b, i, col0 + h)),
                  pl.BlockSpec((1, l, dh), lambda b, h, i: (b, 0, col0 + nh + h)),
                  pl.BlockSpec((1, l, dh), lambda b, h, i: (b, 0, col0 + 2 * nh + h)),
                  pl.BlockSpec((1, NA_WIN_ROWS, GRID_W, NA_WIN_ROWS * GRID_W),
                               lambda b, h, i: (h, 0, 0, 0))],
        out_specs=pl.BlockSpec((1, tq, dh), lambda b, h, i: (b, i, h)),
        out_shape=jax.ShapeDtypeStruct((bsz, l, nh * dh), BF16),
        compiler_params=_cparams(("parallel", "parallel", "arbitrary")),
        name="natten",
    )(pn, pn, pn, bias_tab)


def _outproj_kernel(hm_ref, hn_ref, w1_ref, w2_ref, x_ref, g_ref, o_ref):
    acc = _dot(hm_ref[0], w1_ref[...]) + _dot(hn_ref[0], w2_ref[...])
    o_ref[0] = x_ref[0] + g_ref[0] * acc


def _outproj(hm, hn, w, x, g, tm=512, tn=1024):
    bsz, l, d = x.shape
    kh = hm.shape[2]
    return pl.pallas_call(
        _outproj_kernel,
        grid=(bsz, l // tm, d // tn),
        in_specs=[pl.BlockSpec((1, tm, kh), lambda b, i, j: (b, i, 0)),
                  pl.BlockSpec((1, tm, kh), lambda b, i, j: (b, i, 0)),
                  pl.BlockSpec((kh, tn), lambda b, i, j: (0, j)),
                  pl.BlockSpec((kh, tn), lambda b, i, j: (1, j)),
                  pl.BlockSpec((1, tm, tn), lambda b, i, j: (b, i, j)),
                  pl.BlockSpec((1, 1, tn), lambda b, i, j: (b, 0, j))],
        out_specs=pl.BlockSpec((1, tm, tn), lambda b, i, j: (b, i, j)),
        out_shape=jax.ShapeDtypeStruct((bsz, l, d), F32),
        compiler_params=_cparams(("parallel", "parallel", "parallel")),
        name="outproj",
    )(hm, hn, w, w, x, g)


def _top_values_axis0(s, k):
    vals = []
    for _ in range(k):
        mx = jnp.max(s, axis=0, keepdims=True)
        vals.append(mx)
        s = jnp.where(s == mx, NEG_INF, s)
    return vals


_CAND_PAIRS = [(a, b) for a in range(PEER_TOPK + 1) for b in range(PEER_TOPK + 1)
               if (a + 1) * (b + 1) <= PEER_TOPK + 1]
_CAND_ROWS = -(-len(_CAND_PAIRS) // 8) * 8


def _split3(x):
    hi = x.astype(BF16)
    r = x - hi.astype(F32)
    mid = r.astype(BF16)
    lo = (r - mid.astype(F32)).astype(BF16)
    return hi, mid, lo


def _dot_f32(a, b):
    a0, a1, a2 = _split3(a)
    b0, b1, b2 = _split3(b)
    return (_dot(a0, b0) + (_dot(a0, b1) + _dot(a1, b0))
            + (_dot(a0, b2) + _dot(a1, b1) + _dot(a2, b0)))


def _route_kernel(x_ref, g_ref, sc_ref, sh_ref, wq_ref, k1_ref, k2_ref,
                  h_ref, thr_ref, a_ref, s2_ref, b_ref, h_scr, cand_scr):
    @pl.when(pl.program_id(2) == 0)
    def _():
        h = _ada_norm(x_ref[0], g_ref[...], sc_ref[0], sh_ref[0]).astype(BF16)
        h_scr[...] = h
        h_ref[0] = h

    qt = _dot_nt(wq_ref[...], h_scr[...])
    s1 = _dot_f32(k1_ref[0], qt[:PEER_HALF])
    s2 = _dot_f32(k2_ref[0], qt[PEER_HALF:])
    v1 = _top_values_axis0(s1, PEER_TOPK + 1)
    v2 = _top_values_axis0(s2, PEER_TOPK + 1)
    cand_scr[...] = jnp.full(cand_scr.shape, NEG_INF, F32)
    for i, (ka, kb) in enumerate(_CAND_PAIRS):
        cand_scr[i:i + 1, :] = v1[ka] + v2[kb]
    sc = _top_values_axis0(cand_scr[...], PEER_TOPK + 1)
    z = jnp.ones_like(sc[0])
    for kk in range(1, PEER_TOPK):
        z = z + jnp.exp(sc[kk] - sc[0])
    tau = 0.5 * (sc[PEER_TOPK - 1] + sc[PEER_TOPK])
    thr_ref[0, 0] = tau - s1
    a_ref[0, 0] = jnp.exp(s1 - v1[0]) / z
    s2_ref[0, 0] = s2
    b_ref[0, 0] = jnp.exp(s2 - v2[0])


def _peer_route(x, g, sc, sh, wq_t, k1, k2, tm=512):
    bsz, l, d = x.shape
    nh = PEER_HEADS
    qd = 2 * PEER_HALF
    nk = PEER_N_KEYS
    tab = jax.ShapeDtypeStruct((bsz, nh, nk, l), F32)
    tab_spec = pl.BlockSpec((1, 1, nk, tm), lambda b, i, h: (b, h, 0, i))
    return pl.pallas_call(
        _route_kernel,
        grid=(bsz, l // tm, nh),
        in_specs=[pl.BlockSpec((1, tm, d), lambda b, i, h: (b, i, 0)),
                  pl.BlockSpec((1, d), lambda b, i, h: (0, 0)),
                  pl.BlockSpec((1, 1, d), lambda b, i, h: (b, 0, 0)),
                  pl.BlockSpec((1, 1, d), lambda b, i, h: (b, 0, 0)),
                  pl.BlockSpec((qd, d), lambda b, i, h: (h, 0)),
                  pl.BlockSpec((1, nk, PEER_HALF), lambda b, i, h: (h, 0, 0)),
                  pl.BlockSpec((1, nk, PEER_HALF), lambda b, i, h: (h, 0, 0))],
        out_specs=[pl.BlockSpec((1, tm, d), lambda b, i, h: (b, i, 0)),
                   tab_spec, tab_spec, tab_spec, tab_spec],
        out_shape=[jax.ShapeDtypeStruct((bsz, l, d), BF16), tab, tab, tab, tab],
        scratch_shapes=[pltpu.VMEM((tm, d), BF16), pltpu.VMEM((_CAND_ROWS, tm), F32)],
        compiler_params=_cparams(("parallel", "parallel", "arbitrary")),
        name="peer_route",
    )(x, g, sc, sh, wq_t, k1, k2)


def _gelu_tanh(x):
    return 0.5 * x * (1.0 + jnp.tanh(0.7978845608028654 * (x + 0.044715 * (x * x * x))))


def _expert_kernel(h_ref, u_ref, vt_ref, thr_ref, a_ref, s2_ref, b_ref, o_ref, p_scr, *, te, tm, td):
    j = pl.program_id(2)
    nk = PEER_N_KEYS
    act = _gelu_tanh(_dot_nt(u_ref[...], h_ref[0]))
    for aa in range(te // nk):
        a_idx = j * (te // nk) + aa
        for tc in range(tm // LANES):
            lanes = slice(tc * LANES, (tc + 1) * LANES)
            w = jnp.zeros((nk, LANES), F32)
            thr8 = thr_ref[0, a_idx, :, lanes]
            fa8 = a_ref[0, a_idx, :, lanes]
            for hh in range(PEER_HEADS):
                thr = thr8[hh:hh + 1]
                fa = fa8[hh:hh + 1]
                w = w + jnp.where(s2_ref[0, hh, :, lanes] >= thr, b_ref[0, hh, :, lanes], 0.0) * fa
            p_scr[aa * nk:(aa + 1) * nk, lanes] = (w * act[aa * nk:(aa + 1) * nk, lanes]).astype(BF16)

    @pl.when(j == 0)
    def _():
        o_ref[...] = jnp.zeros_like(o_ref)

    p = p_scr[...]
    for dc in range(vt_ref.shape[0] // td):
        rows = slice(dc * td, (dc + 1) * td)
        o_ref[0, rows, :] += _dot(vt_ref[rows, :], p)


def _peer_experts(h, u, vt, thr, fa, s2, fb, tm=512, te=512, td=512):
    bsz, l, d = h.shape
    ne = u.shape[0]
    nh, nk = PEER_HEADS, PEER_N_KEYS
    tab_spec = pl.BlockSpec((1, nh, nk, tm), lambda b, i, j: (b, 0, 0, i),
                            pipeline_mode=pl.Buffered(1))
    tab_a_spec = pl.BlockSpec((1, nk, nh, tm), lambda b, i, j: (b, 0, 0, i),
                              pipeline_mode=pl.Buffered(1))
    return pl.pallas_call(
        functools.partial(_expert_kernel, te=te, tm=tm, td=td),
        grid=(bsz, l // tm, ne // te),
        in_specs=[pl.BlockSpec((1, tm, d), lambda b, i, j: (b, i, 0), pipeline_mode=pl.Buffered(1)),
                  pl.BlockSpec((te, d), lambda b, i, j: (j, 0)),
                  pl.BlockSpec((d, te), lambda b, i, j: (0, j)),
                  tab_a_spec, tab_a_spec, tab_spec, tab_spec],
        out_specs=pl.BlockSpec((1, d, tm), lambda b, i, j: (b, 0, i)),
        out_shape=jax.ShapeDtypeStruct((bsz, d, l), F32),
        scratch_shapes=[pltpu.VMEM((te, tm), BF16)],
        compiler_params=_cparams(("parallel", "parallel", "arbitrary")),
        name="peer_experts",
    )(h, u, vt, thr, fa, s2, fb)


def _final_kernel(x_ref, pt_ref, g2_ref, fg_ref, o_ref):
    x = x_ref[0] + g2_ref[0] * pt_ref[0].T
    ms = jnp.mean(x * x, axis=-1, keepdims=True)
    o_ref[0] = (x * lax.rsqrt(ms + EPS)) * fg_ref[...]


def _final(x, peer_t, g2, fg, tm=256):
    bsz, l, d = x.shape
    return pl.pallas_call(
        _final_kernel,
        grid=(bsz, l // tm),
        in_specs=[pl.BlockSpec((1, tm, d), lambda b, i: (b, i, 0)),
                  pl.BlockSpec((1, d, tm), lambda b, i: (b, 0, i)),
                  pl.BlockSpec((1, 1, d), lambda b, i: (b, 0, 0)),
                  pl.BlockSpec((1, d), lambda b, i: (0, 0))],
        out_specs=pl.BlockSpec((1, tm, d), lambda b, i: (b, i, 0)),
        out_shape=jax.ShapeDtypeStruct((bsz, l, d), F32),
        compiler_params=_cparams(("parallel", "parallel")),
        name="final_norm",
    )(x, peer_t, g2, fg)


def _prepare_weights(w_in, gate_b, na_rpb, w_out, peer_wq, peer_k1, peer_k2, peer_u, peer_v):
    mw = MLSTM_HEADS * MLSTM_HEAD_DIM
    n_gates = 4 * MLSTM_HEADS
    w_main = jnp.concatenate([w_in[:, :4 * mw], w_in[:, 4 * mw + n_gates:]], axis=1).astype(BF16)
    w_gate = jnp.pad(w_in[:, 4 * mw:4 * mw + n_gates], ((0, 0), (0, LANES - n_gates))).astype(BF16)
    gb = gate_b.reshape(4, MLSTM_HEADS).T
    return dict(
        w_main=w_main, w_gate=w_gate,
        gate_bc=gb[:, None, :], gate_br=gb[:, :, None],
        na_bias=_na_bias_table(na_rpb),
        w_out=w_out.astype(BF16),
        wq_t=peer_wq.T.astype(BF16),
        k1=peer_k1, k2=peer_k2,
        u=peer_u.astype(BF16), vt=peer_v.T.astype(BF16))


def _trunk(x, mod, norm1_g, mlstm_norm_g, norm2_g, final_g, wts):
    bsz, l, d = x.shape
    sh1, sc1, g1, sh2, sc2, g2 = [m[:, None, :] for m in jnp.split(mod, 6, axis=-1)]
    mw = MLSTM_HEADS * MLSTM_HEAD_DIM
    proj, gates = _inproj(x, norm1_g, sc1, sh1, wts["w_main"], wts["w_gate"])
    g4 = gates[:, :, :4 * MLSTM_HEADS].reshape(bsz, l, 4, MLSTM_HEADS)
    gc = g4.transpose(0, 3, 1, 2)
    gr = g4.transpose(0, 3, 2, 1)
    t = min(256, l)
    h_bwd = _mlstm_direction(proj, gc, gr, wts["gate_bc"], wts["gate_br"], True, t=t)
    hm = _mlstm_direction(proj, gc, gr, wts["gate_bc"], wts["gate_br"], False,
                          h_bwd=h_bwd, norm_g=mlstm_norm_g, t=t)
    hn = _neighbourhood_attention(proj, wts["na_bias"], 4 * mw // NA_HEAD_DIM)
    x1 = _outproj(hm, hn, wts["w_out"], x, g1)
    h2, thr, fa, s2, fb = _peer_route(x1, norm2_g, sc2, sh2, wts["wq_t"], wts["k1"], wts["k2"])
    peer_t = _peer_experts(h2, wts["u"], wts["vt"], thr.transpose(0, 2, 1, 3),
                           fa.transpose(0, 2, 1, 3), s2, fb)
    return _final(x1, peer_t, g2, final_g)


def kernel(x_prompt, x_sample, c_prompt, c_sample, ada_w, ada_b, norm1_g, w_in, gate_b, mlstm_norm_g, na_rpb, w_out, norm2_g, peer_wq, peer_k1, peer_k2, peer_u, peer_v, final_g):
    assert ada_w.shape[0] == 1, "single-layer trunk"
    nb_p = c_prompt.shape[0]
    nb_s = c_sample.shape[0]
    rows = -(-(nb_p + nb_s) // 8) * 8
    c_all = jnp.concatenate([c_prompt, c_sample], axis=0)
    c_all = jnp.pad(c_all, ((0, rows - nb_p - nb_s), (0, 0)))
    mod = _modulation(c_all, ada_w[0], ada_b)
    wts = _prepare_weights(w_in[0], gate_b[0], na_rpb[0], w_out[0], peer_wq[0], peer_k1[0],
                           peer_k2[0], peer_u[0], peer_v[0])
    fg = final_g[None, :]
    y_prompt = _trunk(x_prompt, mod[:nb_p], norm1_g, mlstm_norm_g, norm2_g, fg, wts)
    y_sample = _trunk(x_sample, mod[nb_p:nb_p + nb_s], norm1_g, mlstm_norm_g, norm2_g, fg, wts)
    return (y_prompt, y_sample)
```

```python
import functools

import jax
import jax.numpy as jnp
from jax import lax
from jax.experimental import pallas as pl
from jax.experimental.pallas import tpu as pltpu

F32 = jnp.float32
BF16 = jnp.bfloat16

GRID_W = 64
MLSTM_HEADS = 8
MLSTM_HEAD_DIM = 256
NA_HEADS = 16
NA_HEAD_DIM = 128
NA_WIN_ROWS = 8
NA_WIN_COLS = 16
PEER_HEADS = 8
PEER_HALF = 128
PEER_N_KEYS = 128
PEER_TOPK = 16
EPS = 1e-6

LANES = 128
VMEM_LIMIT = 56 * 1024 * 1024

NEG_INF = float("-inf")


def _cparams(sem):
    return pltpu.CompilerParams(dimension_semantics=sem, vmem_limit_bytes=VMEM_LIMIT)


def _dot(a, b):
    return jnp.dot(a, b, preferred_element_type=F32)


def _dot_nt(a, b):
    return lax.dot_general(a, b, (((1,), (1,)), ((), ())), preferred_element_type=F32)


def _mod_kernel(c_ref, w_ref, b_ref, o_ref):
    c = c_ref[...]
    s = c * jax.nn.sigmoid(c)
    hi = s.astype(BF16)
    lo = (s - hi.astype(F32)).astype(BF16)
    w = w_ref[...].astype(BF16)
    o_ref[...] = _dot(hi, w) + _dot(lo, w) + b_ref[...]


def _modulation(c, w, b, tn=512):
    r, d = c.shape
    n = w.shape[1]
    return pl.pallas_call(
        _mod_kernel,
        grid=(n // tn,),
        in_specs=[pl.BlockSpec((r, d), lambda j: (0, 0)),
                  pl.BlockSpec((d, tn), lambda j: (0, j)),
                  pl.BlockSpec((1, tn), lambda j: (0, j))],
        out_specs=pl.BlockSpec((r, tn), lambda j: (0, j)),
        out_shape=jax.ShapeDtypeStruct((r, n), F32),
        compiler_params=_cparams(("parallel",)),
        name="modulation",
    )(c, w, b)


def _ada_norm(x, g, sc, sh):
    ms = jnp.mean(x * x, axis=-1, keepdims=True)
    y = x * lax.rsqrt(ms + EPS)
    return (y * g) * (1.0 + sc) + sh


def _inproj_kernel(x_ref, g_ref, sc_ref, sh_ref, w_ref, wg_ref, o_ref, og_ref, h_scr):
    @pl.when(pl.program_id(2) == 0)
    def _():
        h = _ada_norm(x_ref[0], g_ref[...], sc_ref[0], sh_ref[0]).astype(BF16)
        h_scr[...] = h
        og_ref[0] = _dot(h, wg_ref[...])

    o_ref[0] = _dot(h_scr[...], w_ref[...]).astype(BF16)


def _inproj(x, g, sc, sh, w, wg, tm=512, tn=1024):
    bsz, l, d = x.shape
    n = w.shape[1]
    ng = wg.shape[1]
    return pl.pallas_call(
        _inproj_kernel,
        grid=(bsz, l // tm, n // tn),
        in_specs=[pl.BlockSpec((1, tm, d), lambda b, i, j: (b, i, 0)),
                  pl.BlockSpec((1, d), lambda b, i, j: (0, 0)),
                  pl.BlockSpec((1, 1, d), lambda b, i, j: (b, 0, 0)),
                  pl.BlockSpec((1, 1, d), lambda b, i, j: (b, 0, 0)),
                  pl.BlockSpec((d, tn), lambda b, i, j: (0, j)),
                  pl.BlockSpec((d, ng), lambda b, i, j: (0, 0))],
        out_specs=[pl.BlockSpec((1, tm, tn), lambda b, i, j: (b, i, j)),
                   pl.BlockSpec((1, tm, ng), lambda b, i, j: (b, i, 0))],
        out_shape=[jax.ShapeDtypeStruct((bsz, l, n), BF16),
                   jax.ShapeDtypeStruct((bsz, l, ng), F32)],
        scratch_shapes=[pltpu.VMEM((tm, d), BF16)],
        compiler_params=_cparams(("parallel", "parallel", "arbitrary")),
        name="inproj",
    )(x, g, sc, sh, w, wg)


def _log_sigmoid(x):
    return jnp.minimum(x, 0.0) - jnp.log(1.0 + jnp.exp(-jnp.abs(x)))


def _mlstm_chunk(q, k, v, i_col, f_col, i_row, f_row, c_st, n_st, m_st, reverse):
    t = q.shape[0]
    kscale = MLSTM_HEAD_DIM ** -0.5
    r_i = lax.broadcasted_iota(jnp.int32, (t, t), 0)
    c_i = lax.broadcasted_iota(jnp.int32, (t, t), 1)
    seen = (c_i >= r_i) if reverse else (c_i <= r_i)
    seen_t = (r_i >= c_i) if reverse else (r_i <= c_i)
    b_col = jnp.sum(jnp.where(seen, f_row, 0.0), axis=1, keepdims=True)
    b_row = jnp.sum(jnp.where(seen_t, f_col, 0.0), axis=0, keepdims=True)
    d = jnp.where(seen, b_col - b_row + i_row, NEG_INF)
    inter = b_col + m_st
    m_t = jnp.maximum(inter, jnp.max(d, axis=1, keepdims=True))
    w_inter = jnp.exp(inter - m_t)
    p = jnp.exp(d - m_t) * (_dot_nt(q, k) * kscale)
    num = w_inter * _dot(q, c_st.astype(BF16)) + _dot(p.astype(BF16), v)
    den = (w_inter * jnp.sum(q.astype(F32) * n_st, axis=1, keepdims=True)
           + jnp.sum(p, axis=1, keepdims=True))
    h = num / jnp.maximum(jnp.abs(den), jnp.exp(-m_t))
    b_last = b_col[0:1] if reverse else b_col[t - 1:t]
    dec = b_last - b_col + i_col
    m_new = jnp.maximum(b_last + m_st, jnp.max(dec, axis=0, keepdims=True))
    w_old = jnp.exp(b_last + m_st - m_new)
    w_s = jnp.exp(dec - m_new)
    kw = k.astype(F32) * (w_s * kscale)
    c_new = w_old * c_st + _dot(kw.T.astype(BF16), v)
    n_new = w_old * n_st + jnp.sum(kw, axis=0, keepdims=True)
    return h, c_new, n_new, m_new


def _mlstm_kernel(*refs, reverse, hb):
    if reverse:
        q_ref, k_ref, v_ref, gc_ref, gr_ref, bc_ref, br_ref, o_ref, c_scr, n_scr, m_scr = refs
    else:
        (q_ref, k_ref, v_ref, gc_ref, gr_ref, bc_ref, br_ref, hb_ref, mo_ref, ng_ref,
         o_ref, c_scr, n_scr, m_scr) = refs

    @pl.when(pl.program_id(2) == 0)
    def _():
        c_scr[...] = jnp.zeros_like(c_scr)
        n_scr[...] = jnp.zeros_like(n_scr)
        m_scr[...] = jnp.zeros_like(m_scr)

    dh = MLSTM_HEAD_DIM
    gi = 2 if reverse else 0
    for hh in range(hb):
        cols = slice(hh * dh, (hh + 1) * dh)
        gc = gc_ref[0, hh] + bc_ref[hh]
        gr = gr_ref[0, hh] + br_ref[hh]
        h, c_new, n_new, m_new = _mlstm_chunk(
            q_ref[0, :, cols], k_ref[0, :, cols], v_ref[0, :, cols],
            gc[:, gi:gi + 1], _log_sigmoid(gc[:, gi + 1:gi + 2]),
            gr[gi:gi + 1, :], _log_sigmoid(gr[gi + 1:gi + 2, :]),
            c_scr[hh], n_scr[hh], m_scr[hh], reverse)
        c_scr[hh] = c_new
        n_scr[hh] = n_new
        m_scr[hh] = m_new
        if reverse:
            o_ref[0, :, cols] = h
        else:
            ht = h + hb_ref[0, :, cols]
            ht = ht * lax.rsqrt(jnp.mean(ht * ht, axis=-1, keepdims=True) + EPS)
            ht = ht * ng_ref[:, cols]
            o_ref[0, :, cols] = (ht * jax.nn.sigmoid(mo_ref[0, :, cols].astype(F32))).astype(BF16)


def _mlstm_direction(pm, gc, gr, bc, br, reverse, h_bwd=None, norm_g=None, t=256, hb=2):
    bsz, l, _ = pm.shape
    nh = MLSTM_HEADS
    dh = MLSTM_HEAD_DIM
    w = nh * dh
    nc = l // t
    ng = nh // hb
    wb = hb * dh

    def ci(c):
        return (nc - 1 - c) if reverse else c

    in_specs = [pl.BlockSpec((1, t, wb), lambda b, g, c: (b, ci(c), g)),
                pl.BlockSpec((1, t, wb), lambda b, g, c: (b, ci(c), ng + g)),
                pl.BlockSpec((1, t, wb), lambda b, g, c: (b, ci(c), 2 * ng + g)),
                pl.BlockSpec((1, hb, t, 4), lambda b, g, c: (b, g, ci(c), 0)),
                pl.BlockSpec((1, hb, 4, t), lambda b, g, c: (b, g, 0, ci(c))),
                pl.BlockSpec((hb, 1, 4), lambda b, g, c: (g, 0, 0)),
                pl.BlockSpec((hb, 4, 1), lambda b, g, c: (g, 0, 0))]
    args = [pm, pm, pm, gc, gr, bc, br]
    if not reverse:
        in_specs += [pl.BlockSpec((1, t, wb), lambda b, g, c: (b, c, g)),
                     pl.BlockSpec((1, t, wb), lambda b, g, c: (b, c, 3 * ng + g)),
                     pl.BlockSpec((1, wb), lambda b, g, c: (0, g))]
        args += [h_bwd, pm, norm_g]
    return pl.pallas_call(
        functools.partial(_mlstm_kernel, reverse=reverse, hb=hb),
        grid=(bsz, ng, nc),
        in_specs=in_specs,
        out_specs=pl.BlockSpec((1, t, wb), lambda b, g, c: (b, ci(c), g)),
        out_shape=jax.ShapeDtypeStruct((bsz, l, w), F32 if reverse else BF16),
        scratch_shapes=[pltpu.VMEM((hb, dh, dh), F32),
                        pltpu.VMEM((hb, 1, dh), F32),
                        pltpu.VMEM((hb, 1, 1), F32)],
        compiler_params=_cparams(("parallel", "parallel", "arbitrary")),
        name="mlstm_bwd" if reverse else "mlstm_fwd",
    )(*args)


def _na_bias_table(rpb):
    c = jnp.arange(GRID_W)
    cs = jnp.clip(c - NA_WIN_COLS // 2, 0, GRID_W - NA_WIN_COLS)
    col_ok = (c[None, :] >= cs[:, None]) & (c[None, :] < cs[:, None] + NA_WIN_COLS)
    dc = jnp.clip(c[None, :] - c[:, None], -(NA_WIN_COLS - 1), NA_WIN_COLS - 1) + (NA_WIN_COLS - 1)
    onehot = (dc[None] == jnp.arange(2 * NA_WIN_COLS - 1)[:, None, None]).astype(F32)
    t1 = jnp.einsum("hdc,cqk->hdqk", rpb.astype(F32), onehot, precision=lax.Precision.HIGHEST)
    t1 = jnp.where(col_ok[None, None], t1, NEG_INF)
    fill = jnp.full_like(t1[:, :1], NEG_INF)
    lo = jnp.concatenate([fill, t1], axis=1)
    hi = jnp.concatenate([t1, fill], axis=1)
    return jnp.concatenate([lo, hi], axis=-1)


NA_RBLK = 8
NA_KROWS = 2 * NA_WIN_ROWS


def _na_block(q_ref, kw, vw, tp_ref, o_ref, p_scr, shift, starts):
    w2 = 2 * GRID_W
    scale = NA_HEAD_DIM ** -0.5
    s = _dot_nt(q_ref[0], kw) * scale
    lane = lax.broadcasted_iota(jnp.int32, (GRID_W, w2), 1)
    inv_l = []
    for jr in range(NA_RBLK):
        rows = slice(jr * GRID_W, (jr + 1) * GRID_W)
        st = starts[jr]
        blocks = {}
        for pr in range(NA_KROWS // 2):
            ok0 = st <= 2 * pr < st + NA_WIN_ROWS
            ok1 = st <= 2 * pr + 1 < st + NA_WIN_ROWS
            if not (ok0 or ok1):
                p_scr[rows, pr * w2:(pr + 1) * w2] = jnp.zeros((GRID_W, w2), BF16)
                continue
            d0 = shift + 2 * pr - jr + (NA_WIN_ROWS - 1)
            x = s[rows, pr * w2:(pr + 1) * w2] + tp_ref[0, d0 + 1]
            if not ok0:
                x = jnp.where(lane >= GRID_W, x, NEG_INF)
            if not ok1:
                x = jnp.where(lane < GRID_W, x, NEG_INF)
            blocks[pr] = x
        m = functools.reduce(jnp.maximum, blocks.values())
        m = jnp.max(m, axis=-1, keepdims=True)
        tot = None
        for pr, x in blocks.items():
            p = jnp.exp(x - m)
            tot = p if tot is None else tot + p
            p_scr[rows, pr * w2:(pr + 1) * w2] = p.astype(BF16)
        inv_l.append(1.0 / jnp.sum(tot, axis=-1, keepdims=True))
    o = _dot(p_scr[...], vw)
    for jr in range(NA_RBLK):
        rows = slice(jr * GRID_W, (jr + 1) * GRID_W)
        o_ref[0, rows, :] = (o[rows] * inv_l[jr]).astype(BF16)


def _na_kernel(q_ref, k_ref, v_ref, tp_ref, o_ref, p_scr, *, rows):
    i = pl.program_id(2)
    nblk = rows // NA_RBLK
    half = NA_WIN_ROWS // 2
    kwin = NA_KROWS * GRID_W

    def run(base, shift, starts):
        kpos = base * GRID_W
        if not isinstance(kpos, int):
            kpos = pl.multiple_of(kpos, GRID_W)
        _na_block(q_ref, k_ref[0, pl.ds(kpos, kwin), :], v_ref[0, pl.ds(kpos, kwin), :],
                  tp_ref, o_ref, p_scr, shift, starts)

    @pl.when(i == 0)
    def _():
        run(0, 0, [max(jr - half, 0) for jr in range(NA_RBLK)])

    @pl.when(i == nblk - 1)
    def _():
        run(rows - NA_KROWS, NA_RBLK - NA_KROWS,
            [min(jr + half, NA_KROWS - NA_WIN_ROWS) for jr in range(NA_RBLK)])

    @pl.when(jnp.logical_and(i > 0, i < nblk - 1))
    def _():
        run(i * NA_RBLK - half, -half, list(range(NA_RBLK)))


def _neighbourhood_attention(pn, bias_tab, col0):
    bsz, l, _ = pn.shape
    nh, dh = NA_HEADS, NA_HEAD_DIM
    rows = l // GRID_W
    assert rows >= NA_KROWS and rows % NA_RBLK == 0
    tq = NA_RBLK * GRID_W
    return pl.pallas_call(
        functools.partial(_na_kernel, rows=rows),
        grid=(bsz, nh, rows // NA_RBLK),
        in_specs=[pl.BlockSpec((1, tq, dh), lambda b, h, i: (b, i, col0 + h)),
                  pl.BlockSpec((1, l, dh), lambda b, h, i: (b, 0, col0 + nh + h)),
                  pl.BlockSpec((1, l, dh), lambda b, h, i: (b, 0, col0 + 2 * nh + h)),
                  pl.BlockSpec((1, 2 * NA_WIN_ROWS, GRID_W, 2 * GRID_W), lambda b, h, i: (h, 0, 0, 0))],
        out_specs=pl.BlockSpec((1, tq, dh), lambda b, h, i: (b, i, h)),
        out_shape=jax.ShapeDtypeStruct((bsz, l, nh * dh), BF16),
        scratch_shapes=[pltpu.VMEM((tq, NA_KROWS * GRID_W), BF16)],
        compiler_params=_cparams(("parallel", "parallel", "arbitrary")),
        name="natten",
    )(pn, pn, pn, bias_tab)


def _outproj_kernel(hm_ref, hn_ref, w1_ref, w2_ref, x_ref, g_ref, o_ref):
    acc = _dot(hm_ref[0], w1_ref[...]) + _dot(hn_ref[0], w2_ref[...])
    o_ref[0] = x_ref[0] + g_ref[0] * acc


def _outproj(hm, hn, w, x, g, tm=512, tn=1024):
    bsz, l, d = x.shape
    kh = hm.shape[2]
    return pl.pallas_call(
        _outproj_kernel,
        grid=(bsz, l // tm, d // tn),
        in_specs=[pl.BlockSpec((1, tm, kh), lambda b, i, j: (b, i, 0)),
                  pl.BlockSpec((1, tm, kh), lambda b, i, j: (b, i, 0)),
                  pl.BlockSpec((kh, tn), lambda b, i, j: (0, j)),
                  pl.BlockSpec((kh, tn), lambda b, i, j: (1, j)),
                  pl.BlockSpec((1, tm, tn), lambda b, i, j: (b, i, j)),
                  pl.BlockSpec((1, 1, tn), lambda b, i, j: (b, 0, j))],
        out_specs=pl.BlockSpec((1, tm, tn), lambda b, i, j: (b, i, j)),
        out_shape=jax.ShapeDtypeStruct((bsz, l, d), F32),
        compiler_params=_cparams(("parallel", "parallel", "parallel")),
        name="outproj",
    )(hm, hn, w, w, x, g)


def _top_values_axis0(s, k):
    vals = []
    for _ in range(k):
        mx = jnp.max(s, axis=0, keepdims=True)
        vals.append(mx)
        s = jnp.where(s == mx, NEG_INF, s)
    return vals


_CAND_PAIRS = [(a, b) for a in range(PEER_TOPK + 1) for b in range(PEER_TOPK + 1)
               if (a + 1) * (b + 1) <= PEER_TOPK + 1]
_CAND_ROWS = -(-len(_CAND_PAIRS) // 8) * 8


def _split3(x):
    hi = x.astype(BF16)
    r = x - hi.astype(F32)
    mid = r.astype(BF16)
    lo = (r - mid.astype(F32)).astype(BF16)
    return hi, mid, lo


def _dot_f32(a, b):
    a0, a1, a2 = _split3(a)
    b0, b1, b2 = _split3(b)
    return (_dot(a0, b0) + (_dot(a0, b1) + _dot(a1, b0))
            + (_dot(a0, b2) + _dot(a1, b1) + _dot(a2, b0)))


def _route_kernel(x_ref, g_ref, sc_ref, sh_ref, wq_ref, k1_ref, k2_ref,
                  h_ref, thr_ref, a_ref, s2_ref, b_ref, h_scr, cand_scr):
    @pl.when(pl.program_id(2) == 0)
    def _():
        h = _ada_norm(x_ref[0], g_ref[...], sc_ref[0], sh_ref[0]).astype(BF16)
        h_scr[...] = h
        h_ref[0] = h

    qt = _dot_nt(wq_ref[...], h_scr[...])
    s1 = _dot_f32(k1_ref[0], qt[:PEER_HALF])
    s2 = _dot_f32(k2_ref[0], qt[PEER_HALF:])
    v1 = _top_values_axis0(s1, PEER_TOPK + 1)
    v2 = _top_values_axis0(s2, PEER_TOPK + 1)
    cand_scr[...] = jnp.full(cand_scr.shape, NEG_INF, F32)
    for i, (ka, kb) in enumerate(_CAND_PAIRS):
        cand_scr[i:i + 1, :] = v1[ka] + v2[kb]
    sc = _top_values_axis0(cand_scr[...], PEER_TOPK + 1)
    z = jnp.ones_like(sc[0])
    for kk in range(1, PEER_TOPK):
        z = z + jnp.exp(sc[kk] - sc[0])
    tau = 0.5 * (sc[PEER_TOPK - 1] + sc[PEER_TOPK])
    thr_ref[0, 0] = tau - s1
    a_ref[0, 0] = jnp.exp(s1 - v1[0]) / z
    s2_ref[0, 0] = s2
    b_ref[0, 0] = jnp.exp(s2 - v2[0])


def _peer_route(x, g, sc, sh, wq_t, k1, k2, tm=512):
    bsz, l, d = x.shape
    nh = PEER_HEADS
    qd = 2 * PEER_HALF
    nk = PEER_N_KEYS
    tab = jax.ShapeDtypeStruct((bsz, nh, nk, l), F32)
    tab_spec = pl.BlockSpec((1, 1, nk, tm), lambda b, i, h: (b, h, 0, i))
    return pl.pallas_call(
        _route_kernel,
        grid=(bsz, l // tm, nh),
        in_specs=[pl.BlockSpec((1, tm, d), lambda b, i, h: (b, i, 0)),
                  pl.BlockSpec((1, d), lambda b, i, h: (0, 0)),
                  pl.BlockSpec((1, 1, d), lambda b, i, h: (b, 0, 0)),
                  pl.BlockSpec((1, 1, d), lambda b, i, h: (b, 0, 0)),
                  pl.BlockSpec((qd, d), lambda b, i, h: (h, 0)),
                  pl.BlockSpec((1, nk, PEER_HALF), lambda b, i, h: (h, 0, 0)),
                  pl.BlockSpec((1, nk, PEER_HALF), lambda b, i, h: (h, 0, 0))],
        out_specs=[pl.BlockSpec((1, tm, d), lambda b, i, h: (b, i, 0)),
                   tab_spec, tab_spec, tab_spec, tab_spec],
        out_shape=[jax.ShapeDtypeStruct((bsz, l, d), BF16), tab, tab, tab, tab],
        scratch_shapes=[pltpu.VMEM((tm, d), BF16), pltpu.VMEM((_CAND_ROWS, tm), F32)],
        compiler_params=_cparams(("parallel", "parallel", "arbitrary")),
        name="peer_route",
    )(x, g, sc, sh, wq_t, k1, k2)


def _gelu_tanh(x):
    return 0.5 * x * (1.0 + jnp.tanh(0.7978845608028654 * (x + 0.044715 * (x * x * x))))


EXPERT_ROUNDS = 8


def _expert_kernel(h_ref, u_ref, vt_ref, thr_ref, a_ref, s2_ref, b_ref, o_ref,
                   act_scr, w_scr, p_scr, *, te, tm):
    j = pl.program_id(2)
    nt = pl.num_programs(2) - 2
    nk = PEER_N_KEYS
    cur = j % 2
    prv = 1 - cur
    jt = jnp.minimum(j, nt - 1)
    d = vt_ref.shape[0]

    @pl.when(j == 0)
    def _():
        o_ref[...] = jnp.zeros_like(o_ref)
        act_scr[...] = jnp.zeros_like(act_scr)
        w_scr[...] = jnp.zeros_like(w_scr)
        p_scr[...] = jnp.zeros_like(p_scr)

    nr = EXPERT_ROUNDS
    rd, re = d // nr, te // nr
    assert nk % re == 0
    for r in range(nr):
        rows = slice(r * rd, (r + 1) * rd)
        o_ref[0, rows, :] += _dot(vt_ref[rows, :], p_scr[cur])
        a_idx = jt * (te // nk) + (r * re) // nk
        ks = slice((r * re) % nk, (r * re) % nk + re)
        for tc in range(tm // LANES):
            lanes = slice(tc * LANES, (tc + 1) * LANES)
            thr8 = thr_ref[0, a_idx, :, lanes]
            fa8 = a_ref[0, a_idx, :, lanes]
            w = jnp.zeros((re, LANES), F32)
            for hh in range(PEER_HEADS):
                sel = s2_ref[0, hh, ks, lanes] >= thr8[hh:hh + 1]
                w = w + jnp.where(sel, b_ref[0, hh, ks, lanes], 0.0) * fa8[hh:hh + 1]
            w_scr[cur, r * re:(r + 1) * re, lanes] = w
        er = slice(r * re, (r + 1) * re)
        p_scr[prv, er, :] = (w_scr[prv, er, :] * _gelu_tanh(act_scr[prv, er, :])).astype(BF16)

    @pl.when(j < nt)
    def _():
        act_scr[cur] = _dot_nt(u_ref[...], h_ref[0])


def _peer_experts(h, u, vt, thr, fa, s2, fb, tm=512, te=512):
    bsz, l, d = h.shape
    ne = u.shape[0]
    nh, nk = PEER_HEADS, PEER_N_KEYS
    nt = ne // te
    tab_spec = pl.BlockSpec((1, nh, nk, tm), lambda b, i, j: (b, 0, 0, i),
                            pipeline_mode=pl.Buffered(1))
    tab_a_spec = pl.BlockSpec((1, nk, nh, tm), lambda b, i, j: (b, 0, 0, i),
                              pipeline_mode=pl.Buffered(1))
    return pl.pallas_call(
        functools.partial(_expert_kernel, te=te, tm=tm),
        grid=(bsz, l // tm, nt + 2),
        in_specs=[pl.BlockSpec((1, tm, d), lambda b, i, j: (b, i, 0), pipeline_mode=pl.Buffered(1)),
                  pl.BlockSpec((te, d), lambda b, i, j: (jnp.minimum(j, nt - 1), 0)),
                  pl.BlockSpec((d, te), lambda b, i, j: (0, jnp.maximum(j - 2, 0))),
                  tab_a_spec, tab_a_spec, tab_spec, tab_spec],
        out_specs=pl.BlockSpec((1, d, tm), lambda b, i, j: (b, 0, i)),
        out_shape=jax.ShapeDtypeStruct((bsz, d, l), F32),
        scratch_shapes=[pltpu.VMEM((2, te, tm), F32), pltpu.VMEM((2, te, tm), F32),
                        pltpu.VMEM((2, te, tm), BF16)],
        compiler_params=_cparams(("parallel", "parallel", "arbitrary")),
        name="peer_experts",
    )(h, u, vt, thr, fa, s2, fb)


def _final_kernel(x_ref, pt_ref, g2_ref, fg_ref, o_ref):
    x = x_ref[0] + g2_ref[0] * pt_ref[0].T
    ms = jnp.mean(x * x, axis=-1, keepdims=True)
    o_ref[0] = (x * lax.rsqrt(ms + EPS)) * fg_ref[...]


def _final(x, peer_t, g2, fg, tm=256):
    bsz, l, d = x.shape
    return pl.pallas_call(
        _final_kernel,
        grid=(bsz, l // tm),
        in_specs=[pl.BlockSpec((1, tm, d), lambda b, i: (b, i, 0)),
                  pl.BlockSpec((1, d, tm), lambda b, i: (b, 0, i)),
                  pl.BlockSpec((1, 1, d), lambda b, i: (b, 0, 0)),
                  pl.BlockSpec((1, d), lambda b, i: (0, 0))],
        out_specs=pl.BlockSpec((1, tm, d), lambda b, i: (b, i, 0)),
        out_shape=jax.ShapeDtypeStruct((bsz, l, d), F32),
        compiler_params=_cparams(("parallel", "parallel")),
        name="final_norm",
    )(x, peer_t, g2, fg)


def _prepare_weights(w_in, gate_b, na_rpb, w_out, peer_wq, peer_k1, peer_k2, peer_u, peer_v):
    mw = MLSTM_HEADS * MLSTM_HEAD_DIM
    n_gates = 4 * MLSTM_HEADS
    w_main = jnp.concatenate([w_in[:, :4 * mw], w_in[:, 4 * mw + n_gates:]], axis=1).astype(BF16)
    w_gate = jnp.pad(w_in[:, 4 * mw:4 * mw + n_gates], ((0, 0), (0, LANES - n_gates))).astype(BF16)
    gb = gate_b.reshape(4, MLSTM_HEADS).T
    return dict(
        w_main=w_main, w_gate=w_gate,
        gate_bc=gb[:, None, :], gate_br=gb[:, :, None],
        na_bias=_na_bias_table(na_rpb),
        w_out=w_out.astype(BF16),
        wq_t=peer_wq.T.astype(BF16),
        k1=peer_k1, k2=peer_k2,
        u=peer_u.astype(BF16), vt=peer_v.T.astype(BF16))


def _trunk(x, mod, norm1_g, mlstm_norm_g, norm2_g, final_g, wts):
    bsz, l, d = x.shape
    sh1, sc1, g1, sh2, sc2, g2 = [m[:, None, :] for m in jnp.split(mod, 6, axis=-1)]
    mw = MLSTM_HEADS * MLSTM_HEAD_DIM
    proj, gates = _inproj(x, norm1_g, sc1, sh1, wts["w_main"], wts["w_gate"])
    g4 = gates[:, :, :4 * MLSTM_HEADS].reshape(bsz, l, 4, MLSTM_HEADS)
    gc = g4.transpose(0, 3, 1, 2)
    gr = g4.transpose(0, 3, 2, 1)
    t = min(256, l)
    h_bwd = _mlstm_direction(proj, gc, gr, wts["gate_bc"], wts["gate_br"], True, t=t)
    hm = _mlstm_direction(proj, gc, gr, wts["gate_bc"], wts["gate_br"], False,
                          h_bwd=h_bwd, norm_g=mlstm_norm_g, t=t)
    hn = _neighbourhood_attention(proj, wts["na_bias"], 4 * mw // NA_HEAD_DIM)
    x1 = _outproj(hm, hn, wts["w_out"], x, g1)
    h2, thr, fa, s2, fb = _peer_route(x1, norm2_g, sc2, sh2, wts["wq_t"], wts["k1"], wts["k2"])
    peer_t = _peer_experts(h2, wts["u"], wts["vt"], thr.transpose(0, 2, 1, 3),
                           fa.transpose(0, 2, 1, 3), s2, fb)
    return _final(x1, peer_t, g2, final_g)


def kernel(x_prompt, x_sample, c_prompt, c_sample, ada_w, ada_b, norm1_g, w_in, gate_b, mlstm_norm_g, na_rpb, w_out, norm2_g, peer_wq, peer_k1, peer_k2, peer_u, peer_v, final_g):
    assert ada_w.shape[0] == 1, "single-layer trunk"
    nb_p = c_prompt.shape[0]
    nb_s = c_sample.shape[0]
    rows = -(-(nb_p + nb_s) // 8) * 8
    c_all = jnp.concatenate([c_prompt, c_sample], axis=0)
    c_all = jnp.pad(c_all, ((0, rows - nb_p - nb_s), (0, 0)))
    mod = _modulation(c_all, ada_w[0], ada_b)
    wts = _prepare_weights(w_in[0], gate_b[0], na_rpb[0], w_out[0], peer_wq[0], peer_k1[0],
                           peer_k2[0], peer_u[0], peer_v[0])
    fg = final_g[None, :]
    y_prompt = _trunk(x_prompt, mod[:nb_p], norm1_g, mlstm_norm_g, norm2_g, fg, wts)
    y_sample = _trunk(x_sample, mod[nb_p:nb_p + nb_s], norm1_g, mlstm_norm_g, norm2_g, fg, wts)
    return (y_prompt, y_sample)
```

```python
import functools

import jax
import jax.numpy as jnp
from jax import lax
from jax.experimental import pallas as pl
from jax.experimental.pallas import tpu as pltpu

F32 = jnp.float32
BF16 = jnp.bfloat16

GRID_W = 64
MLSTM_HEADS = 8
MLSTM_HEAD_DIM = 256
NA_HEADS = 16
NA_HEAD_DIM = 128
NA_WIN_ROWS = 8
NA_WIN_COLS = 16
PEER_HEADS = 8
PEER_HALF = 128
PEER_N_KEYS = 128
PEER_TOPK = 16
EPS = 1e-6

LANES = 128
BF16_ROWS = 16
VMEM_LIMIT = 56 * 1024 * 1024

NEG_INF = float("-inf")


def _cparams(sem):
    return pltpu.CompilerParams(dimension_semantics=sem, vmem_limit_bytes=VMEM_LIMIT)


def _dot(a, b):
    return jnp.dot(a, b, preferred_element_type=F32)


def _dot_nt(a, b):
    return lax.dot_general(a, b, (((1,), (1,)), ((), ())), preferred_element_type=F32)


def _mod_kernel(c_ref, w_ref, b_ref, o_ref):
    c = c_ref[...]
    s = c * jax.nn.sigmoid(c)
    hi = s.astype(BF16)
    lo = (s - hi.astype(F32)).astype(BF16)
    w = w_ref[...].astype(BF16)
    o_ref[...] = _dot(hi, w) + _dot(lo, w) + b_ref[...]


def _modulation(c, w, b, tn=512):
    r, d = c.shape
    n = w.shape[1]
    return pl.pallas_call(
        _mod_kernel,
        grid=(n // tn,),
        in_specs=[pl.BlockSpec((r, d), lambda j: (0, 0)),
                  pl.BlockSpec((d, tn), lambda j: (0, j)),
                  pl.BlockSpec((1, tn), lambda j: (0, j))],
        out_specs=pl.BlockSpec((r, tn), lambda j: (0, j)),
        out_shape=jax.ShapeDtypeStruct((r, n), F32),
        compiler_params=_cparams(("parallel",)),
        name="modulation",
    )(c, w, b)


def _ada_norm(x, g, sc, sh):
    ms = jnp.mean(x * x, axis=-1, keepdims=True)
    y = x * lax.rsqrt(ms + EPS)
    return (y * g) * (1.0 + sc) + sh


def _inproj_kernel(x_ref, g_ref, sc_ref, sh_ref, w_ref, wg_ref, o_ref, og_ref, h_scr):
    @pl.when(pl.program_id(2) == 0)
    def _():
        h = _ada_norm(x_ref[0], g_ref[...], sc_ref[0], sh_ref[0]).astype(BF16)
        h_scr[...] = h
        og_ref[0] = _dot(h, wg_ref[...])

    o_ref[0] = _dot(h_scr[...], w_ref[...]).astype(BF16)


def _inproj(x, g, sc, sh, w, wg, tm=512, tn=1024):
    bsz, l, d = x.shape
    n = w.shape[1]
    ng = wg.shape[1]
    return pl.pallas_call(
        _inproj_kernel,
        grid=(bsz, l // tm, n // tn),
        in_specs=[pl.BlockSpec((1, tm, d), lambda b, i, j: (b, i, 0)),
                  pl.BlockSpec((1, d), lambda b, i, j: (0, 0)),
                  pl.BlockSpec((1, 1, d), lambda b, i, j: (b, 0, 0)),
                  pl.BlockSpec((1, 1, d), lambda b, i, j: (b, 0, 0)),
                  pl.BlockSpec((d, tn), lambda b, i, j: (0, j)),
                  pl.BlockSpec((d, ng), lambda b, i, j: (0, 0))],
        out_specs=[pl.BlockSpec((1, tm, tn), lambda b, i, j: (b, i, j)),
                   pl.BlockSpec((1, tm, ng), lambda b, i, j: (b, i, 0))],
        out_shape=[jax.ShapeDtypeStruct((bsz, l, n), BF16),
                   jax.ShapeDtypeStruct((bsz, l, ng), F32)],
        scratch_shapes=[pltpu.VMEM((tm, d), BF16)],
        compiler_params=_cparams(("parallel", "parallel", "arbitrary")),
        name="inproj",
    )(x, g, sc, sh, w, wg)


def _log_sigmoid(x):
    return jnp.minimum(x, 0.0) - jnp.log(1.0 + jnp.exp(-jnp.abs(x)))


def _mlstm_chunk(q, k, v, i_col, f_col, i_row, f_row, c_st, n_st, m_st, reverse):
    t = q.shape[0]
    kscale = MLSTM_HEAD_DIM ** -0.5
    r_i = lax.broadcasted_iota(jnp.int32, (t, t), 0)
    c_i = lax.broadcasted_iota(jnp.int32, (t, t), 1)
    seen = (c_i >= r_i) if reverse else (c_i <= r_i)
    seen_t = (r_i >= c_i) if reverse else (r_i <= c_i)
    b_col = jnp.sum(jnp.where(seen, f_row, 0.0), axis=1, keepdims=True)
    b_row = jnp.sum(jnp.where(seen_t, f_col, 0.0), axis=0, keepdims=True)
    d = jnp.where(seen, b_col - b_row + i_row, NEG_INF)
    inter = b_col + m_st
    m_t = jnp.maximum(inter, jnp.max(d, axis=1, keepdims=True))
    w_inter = jnp.exp(inter - m_t)
    p = jnp.exp(d - m_t) * (_dot_nt(q, k) * kscale)
    num = w_inter * _dot(q, c_st.astype(BF16)) + _dot(p.astype(BF16), v)
    den = (w_inter * jnp.sum(q.astype(F32) * n_st, axis=1, keepdims=True)
           + jnp.sum(p, axis=1, keepdims=True))
    h = num / jnp.maximum(jnp.abs(den), jnp.exp(-m_t))
    b_last = b_col[0:1] if reverse else b_col[t - 1:t]
    dec = b_last - b_col + i_col
    m_new = jnp.maximum(b_last + m_st, jnp.max(dec, axis=0, keepdims=True))
    w_old = jnp.exp(b_last + m_st - m_new)
    w_s = jnp.exp(dec - m_new)
    kw = k.astype(F32) * (w_s * kscale)
    c_new = w_old * c_st + _dot(kw.T.astype(BF16), v)
    n_new = w_old * n_st + jnp.sum(kw, axis=0, keepdims=True)
    return h, c_new, n_new, m_new


def _mlstm_kernel(*refs, reverse, hb):
    if reverse:
        q_ref, k_ref, v_ref, gc_ref, gr_ref, bc_ref, br_ref, o_ref, c_scr, n_scr, m_scr = refs
    else:
        (q_ref, k_ref, v_ref, gc_ref, gr_ref, bc_ref, br_ref, hb_ref, mo_ref, ng_ref,
         o_ref, c_scr, n_scr, m_scr) = refs

    @pl.when(pl.program_id(2) == 0)
    def _():
        c_scr[...] = jnp.zeros_like(c_scr)
        n_scr[...] = jnp.zeros_like(n_scr)
        m_scr[...] = jnp.zeros_like(m_scr)

    dh = MLSTM_HEAD_DIM
    gi = 2 if reverse else 0
    for hh in range(hb):
        cols = slice(hh * dh, (hh + 1) * dh)
        gc = gc_ref[0, hh] + bc_ref[hh]
        gr = gr_ref[0, hh] + br_ref[hh]
        h, c_new, n_new, m_new = _mlstm_chunk(
            q_ref[0, :, cols], k_ref[0, :, cols], v_ref[0, :, cols],
            gc[:, gi:gi + 1], _log_sigmoid(gc[:, gi + 1:gi + 2]),
            gr[gi:gi + 1, :], _log_sigmoid(gr[gi + 1:gi + 2, :]),
            c_scr[hh], n_scr[hh], m_scr[hh], reverse)
        c_scr[hh] = c_new
        n_scr[hh] = n_new
        m_scr[hh] = m_new
        if reverse:
            o_ref[0, :, cols] = h
        else:
            ht = h + hb_ref[0, :, cols]
            ht = ht * lax.rsqrt(jnp.mean(ht * ht, axis=-1, keepdims=True) + EPS)
            ht = ht * ng_ref[:, cols]
            o_ref[0, :, cols] = (ht * jax.nn.sigmoid(mo_ref[0, :, cols].astype(F32))).astype(BF16)


def _mlstm_direction(pm, gc, gr, bc, br, reverse, h_bwd=None, norm_g=None, t=256, hb=2):
    bsz, l, _ = pm.shape
    nh = MLSTM_HEADS
    dh = MLSTM_HEAD_DIM
    w = nh * dh
    nc = l // t
    ng = nh // hb
    wb = hb * dh

    def ci(c):
        return (nc - 1 - c) if reverse else c

    in_specs = [pl.BlockSpec((1, t, wb), lambda b, g, c: (b, ci(c), g)),
                pl.BlockSpec((1, t, wb), lambda b, g, c: (b, ci(c), ng + g)),
                pl.BlockSpec((1, t, wb), lambda b, g, c: (b, ci(c), 2 * ng + g)),
                pl.BlockSpec((1, hb, t, 4), lambda b, g, c: (b, g, ci(c), 0)),
                pl.BlockSpec((1, hb, 4, t), lambda b, g, c: (b, g, 0, ci(c))),
                pl.BlockSpec((hb, 1, 4), lambda b, g, c: (g, 0, 0)),
                pl.BlockSpec((hb, 4, 1), lambda b, g, c: (g, 0, 0))]
    args = [pm, pm, pm, gc, gr, bc, br]
    if not reverse:
        in_specs += [pl.BlockSpec((1, t, wb), lambda b, g, c: (b, c, g)),
                     pl.BlockSpec((1, t, wb), lambda b, g, c: (b, c, 3 * ng + g)),
                     pl.BlockSpec((1, wb), lambda b, g, c: (0, g))]
        args += [h_bwd, pm, norm_g]
    return pl.pallas_call(
        functools.partial(_mlstm_kernel, reverse=reverse, hb=hb),
        grid=(bsz, ng, nc),
        in_specs=in_specs,
        out_specs=pl.BlockSpec((1, t, wb), lambda b, g, c: (b, ci(c), g)),
        out_shape=jax.ShapeDtypeStruct((bsz, l, w), F32 if reverse else BF16),
        scratch_shapes=[pltpu.VMEM((hb, dh, dh), F32),
                        pltpu.VMEM((hb, 1, dh), F32),
                        pltpu.VMEM((hb, 1, 1), F32)],
        compiler_params=_cparams(("parallel", "parallel", "arbitrary")),
        name="mlstm_bwd" if reverse else "mlstm_fwd",
    )(*args)


def _na_bias_table(rpb):
    c = jnp.arange(GRID_W)
    cs = jnp.clip(c - NA_WIN_COLS // 2, 0, GRID_W - NA_WIN_COLS)
    col_ok = (c[None, :] >= cs[:, None]) & (c[None, :] < cs[:, None] + NA_WIN_COLS)
    dc = jnp.clip(c[None, :] - c[:, None], -(NA_WIN_COLS - 1), NA_WIN_COLS - 1) + (NA_WIN_COLS - 1)
    onehot = (dc[None] == jnp.arange(2 * NA_WIN_COLS - 1)[:, None, None]).astype(F32)
    t1 = jnp.einsum("hdc,cqk->hdqk", rpb.astype(F32), onehot, precision=lax.Precision.HIGHEST)
    t1 = jnp.where(col_ok[None, None], t1, NEG_INF)
    fill = jnp.full_like(t1[:, :1], NEG_INF)
    lo = jnp.concatenate([fill, t1], axis=1)
    hi = jnp.concatenate([t1, fill], axis=1)
    return jnp.concatenate([lo, hi], axis=-1)


NA_RBLK = 8
NA_KROWS = 2 * NA_WIN_ROWS


def _na_block(q_ref, kw, vw, tp_ref, o_ref, p_scr, shift, starts):
    w2 = 2 * GRID_W
    scale = NA_HEAD_DIM ** -0.5
    s = _dot_nt(q_ref[0], kw) * scale
    lane = lax.broadcasted_iota(jnp.int32, (GRID_W, w2), 1)
    inv_l = []
    for jr in range(NA_RBLK):
        rows = slice(jr * GRID_W, (jr + 1) * GRID_W)
        st = starts[jr]
        blocks = {}
        for pr in range(NA_KROWS // 2):
            ok0 = st <= 2 * pr < st + NA_WIN_ROWS
            ok1 = st <= 2 * pr + 1 < st + NA_WIN_ROWS
            if not (ok0 or ok1):
                p_scr[rows, pr * w2:(pr + 1) * w2] = jnp.zeros((GRID_W, w2), BF16)
                continue
            d0 = shift + 2 * pr - jr + (NA_WIN_ROWS - 1)
            x = s[rows, pr * w2:(pr + 1) * w2] + tp_ref[0, d0 + 1]
            if not ok0:
                x = jnp.where(lane >= GRID_W, x, NEG_INF)
            if not ok1:
                x = jnp.where(lane < GRID_W, x, NEG_INF)
            blocks[pr] = x
        m = functools.reduce(jnp.maximum, blocks.values())
        m = jnp.max(m, axis=-1, keepdims=True)
        tot = None
        for pr, x in blocks.items():
            p = jnp.exp(x - m)
            tot = p if tot is None else tot + p
            p_scr[rows, pr * w2:(pr + 1) * w2] = p.astype(BF16)
        inv_l.append(1.0 / jnp.sum(tot, axis=-1, keepdims=True))
    o = _dot(p_scr[...], vw)
    for jr in range(NA_RBLK):
        rows = slice(jr * GRID_W, (jr + 1) * GRID_W)
        o_ref[0, rows, :] = (o[rows] * inv_l[jr]).astype(BF16)


def _na_kernel(q_ref, k_ref, v_ref, tp_ref, o_ref, p_scr, *, rows):
    i = pl.program_id(2)
    nblk = rows // NA_RBLK
    half = NA_WIN_ROWS // 2
    kwin = NA_KROWS * GRID_W

    def run(base, shift, starts):
        kpos = base * GRID_W
        if not isinstance(kpos, int):
            kpos = pl.multiple_of(kpos, GRID_W)
        _na_block(q_ref, k_ref[0, pl.ds(kpos, kwin), :], v_ref[0, pl.ds(kpos, kwin), :],
                  tp_ref, o_ref, p_scr, shift, starts)

    @pl.when(i == 0)
    def _():
        run(0, 0, [max(jr - half, 0) for jr in range(NA_RBLK)])

    @pl.when(i == nblk - 1)
    def _():
        run(rows - NA_KROWS, NA_RBLK - NA_KROWS,
            [min(jr + half, NA_KROWS - NA_WIN_ROWS) for jr in range(NA_RBLK)])

    @pl.when(jnp.logical_and(i > 0, i < nblk - 1))
    def _():
        run(i * NA_RBLK - half, -half, list(range(NA_RBLK)))


def _neighbourhood_attention(pn, bias_tab, col0):
    bsz, l, _ = pn.shape
    nh, dh = NA_HEADS, NA_HEAD_DIM
    rows = l // GRID_W
    assert rows >= NA_KROWS and rows % NA_RBLK == 0
    tq = NA_RBLK * GRID_W
    return pl.pallas_call(
        functools.partial(_na_kernel, rows=rows),
        grid=(bsz, nh, rows // NA_RBLK),
        in_specs=[pl.BlockSpec((1, tq, dh), lambda b, h, i: (b, i, col0 + h)),
                  pl.BlockSpec((1, l, dh), lambda b, h, i: (b, 0, col0 + nh + h)),
                  pl.BlockSpec((1, l, dh), lambda b, h, i: (b, 0, col0 + 2 * nh + h)),
                  pl.BlockSpec((1, 2 * NA_WIN_ROWS, GRID_W, 2 * GRID_W), lambda b, h, i: (h, 0, 0, 0))],
        out_specs=pl.BlockSpec((1, tq, dh), lambda b, h, i: (b, i, h)),
        out_shape=jax.ShapeDtypeStruct((bsz, l, nh * dh), BF16),
        scratch_shapes=[pltpu.VMEM((tq, NA_KROWS * GRID_W), BF16)],
        compiler_params=_cparams(("parallel", "parallel", "arbitrary")),
        name="natten",
    )(pn, pn, pn, bias_tab)


def _outproj_kernel(hm_ref, hn_ref, w1_ref, w2_ref, x_ref, g_ref, o_ref):
    acc = _dot(hm_ref[0], w1_ref[...]) + _dot(hn_ref[0], w2_ref[...])
    o_ref[0] = x_ref[0] + g_ref[0] * acc


def _outproj(hm, hn, w, x, g, tm=512, tn=1024):
    bsz, l, d = x.shape
    kh = hm.shape[2]
    return pl.pallas_call(
        _outproj_kernel,
        grid=(bsz, l // tm, d // tn),
        in_specs=[pl.BlockSpec((1, tm, kh), lambda b, i, j: (b, i, 0)),
                  pl.BlockSpec((1, tm, kh), lambda b, i, j: (b, i, 0)),
                  pl.BlockSpec((kh, tn), lambda b, i, j: (0, j)),
                  pl.BlockSpec((kh, tn), lambda b, i, j: (1, j)),
                  pl.BlockSpec((1, tm, tn), lambda b, i, j: (b, i, j)),
                  pl.BlockSpec((1, 1, tn), lambda b, i, j: (b, 0, j))],
        out_specs=pl.BlockSpec((1, tm, tn), lambda b, i, j: (b, i, j)),
        out_shape=jax.ShapeDtypeStruct((bsz, l, d), F32),
        compiler_params=_cparams(("parallel", "parallel", "parallel")),
        name="outproj",
    )(hm, hn, w, w, x, g)


def _top_values_axis0(chunks, k, group):
    vals = [[] for _ in chunks]
    for g0 in range(0, len(chunks), group):
        ids = range(g0, min(g0 + group, len(chunks)))
        state = {c: chunks[c] for c in ids}
        for _ in range(k):
            for c in ids:
                mx = jnp.max(state[c], axis=0, keepdims=True)
                vals[c].append(mx)
                state[c] = jnp.where(state[c] == mx, NEG_INF, state[c])
    return vals


def _count_reaching(rows, x, strict):
    cnt = jnp.zeros_like(x)
    for row in rows:
        cnt = cnt + jnp.where((row > x) if strict else (row >= x), 1.0, 0.0)
    return cnt


_CAND_PAIRS = [(a, b) for a in range(PEER_TOPK + 1) for b in range(PEER_TOPK + 1)
               if (a + 1) * (b + 1) <= PEER_TOPK + 1]
_CAND_ROWS = -(-len(_CAND_PAIRS) // 8) * 8


def _split2(x):
    hi = x.astype(BF16)
    return hi, (x - hi.astype(F32)).astype(BF16)


def _dot_f32(a, b):
    a0, a1 = _split2(a)
    b0, b1 = _split2(b)
    return _dot(a0, b0) + (_dot(a0, b1) + _dot(a1, b0))


def _route_kernel(x_ref, g_ref, sc_ref, sh_ref, wq_ref, k1_ref, k2_ref,
                  ht_ref, cnt_ref, a_ref, r2_ref, b_ref, cand_scr):
    @pl.when(pl.program_id(2) == 0)
    def _():
        h = _ada_norm(x_ref[0], g_ref[...], sc_ref[0], sh_ref[0])
        ht_ref[0] = h.T.astype(BF16)

    qt = _dot(wq_ref[...], ht_ref[0])
    s1 = _dot_f32(k1_ref[0], qt[:PEER_HALF])
    s2 = _dot_f32(k2_ref[0], qt[PEER_HALF:])
    nchunk = s1.shape[1] // LANES
    lanes = [slice(c * LANES, (c + 1) * LANES) for c in range(nchunk)]
    s1c = [s1[:, ln] for ln in lanes]
    s2c = [s2[:, ln] for ln in lanes]
    ktop = PEER_TOPK + 1
    v1 = _top_values_axis0(s1c, ktop, group=2)
    v2 = _top_values_axis0(s2c, ktop, group=2)
    cand_scr[...] = jnp.full(cand_scr.shape, NEG_INF, F32)
    for c in range(nchunk):
        for i, (ka, kb) in enumerate(_CAND_PAIRS):
            cand_scr[i:i + 1, lanes[c]] = v1[c][ka] + v2[c][kb]
    sc = _top_values_axis0([cand_scr[:, ln] for ln in lanes], ktop, group=nchunk)
    for c in range(nchunk):
        z = jnp.ones_like(sc[c][0])
        for kk in range(1, PEER_TOPK):
            z = z + jnp.exp(sc[c][kk] - sc[c][0])
        thr = 0.5 * (sc[c][PEER_TOPK - 1] + sc[c][PEER_TOPK]) - s1c[c]
        cnt_ref[0, 0, :, lanes[c]] = _count_reaching(v2[c], thr, strict=False)
        r2_ref[0, 0, :, lanes[c]] = _count_reaching(v2[c], s2c[c], strict=True).astype(BF16)
        a_ref[0, 0, :, lanes[c]] = jnp.exp(s1c[c] - v1[c][0]) / z
        b_ref[0, 0, :, lanes[c]] = jnp.exp(s2c[c] - v2[c][0]).astype(BF16)


def _peer_route(x, g, sc, sh, wq_t, k1, k2, tm=512):
    bsz, l, d = x.shape
    nh = PEER_HEADS
    qd = 2 * PEER_HALF
    nk = PEER_N_KEYS
    tab = jax.ShapeDtypeStruct((bsz, nh, nk, l), F32)
    tab16 = jax.ShapeDtypeStruct((bsz, nh, nk, l), BF16)
    tab_spec = pl.BlockSpec((1, 1, nk, tm), lambda b, i, h: (b, h, 0, i))
    return pl.pallas_call(
        _route_kernel,
        grid=(bsz, l // tm, nh),
        in_specs=[pl.BlockSpec((1, tm, d), lambda b, i, h: (b, i, 0)),
                  pl.BlockSpec((1, d), lambda b, i, h: (0, 0)),
                  pl.BlockSpec((1, 1, d), lambda b, i, h: (b, 0, 0)),
                  pl.BlockSpec((1, 1, d), lambda b, i, h: (b, 0, 0)),
                  pl.BlockSpec((qd, d), lambda b, i, h: (h, 0)),
                  pl.BlockSpec((1, nk, PEER_HALF), lambda b, i, h: (h, 0, 0)),
                  pl.BlockSpec((1, nk, PEER_HALF), lambda b, i, h: (h, 0, 0))],
        out_specs=[pl.BlockSpec((1, d, tm), lambda b, i, h: (b, 0, i)),
                   tab_spec, tab_spec, tab_spec, tab_spec],
        out_shape=[jax.ShapeDtypeStruct((bsz, d, l), BF16), tab, tab, tab16, tab16],
        scratch_shapes=[pltpu.VMEM((_CAND_ROWS, tm), F32)],
        compiler_params=_cparams(("parallel", "parallel", "arbitrary")),
        name="peer_route",
    )(x, g, sc, sh, wq_t, k1, k2)


def _gelu_tanh(x):
    return 0.5 * x * (1.0 + jnp.tanh(0.7978845608028654 * (x + 0.044715 * (x * x * x))))


EXPERT_ROUNDS = 8


def _expert_kernel(ht_ref, u_ref, vt_ref, cnt_ref, a_ref, r2_ref, b_ref, o_ref,
                   act_scr, w_scr, p_scr, *, te, tm):
    j = pl.program_id(2)
    nt = pl.num_programs(2) - 2
    nk = PEER_N_KEYS
    cur = j % 2
    prv = 1 - cur
    jt = jnp.minimum(j, nt - 1)
    d = vt_ref.shape[1]

    @pl.when(j == 0)
    def _():
        o_ref[...] = jnp.zeros_like(o_ref)
        act_scr[...] = jnp.zeros_like(act_scr)
        w_scr[...] = jnp.zeros_like(w_scr)
        p_scr[...] = jnp.zeros_like(p_scr)

    nr = EXPERT_ROUNDS
    rd, re = d // nr, te // nr
    assert nk % re == 0
    for r in range(nr):
        rows = slice(r * rd, (r + 1) * rd)
        o_ref[0, rows, :] += _dot(vt_ref[0, rows, :], p_scr[cur])
        a_idx = jt * (te // nk) + (r * re) // nk
        k0 = (r * re) % nk
        for tc in range(tm // LANES):
            lanes = slice(tc * LANES, (tc + 1) * LANES)
            cnt8 = cnt_ref[0, a_idx, :, lanes]
            fa8 = a_ref[0, a_idx, :, lanes]
            w = [jnp.zeros((BF16_ROWS, LANES), BF16) for _ in range(re // BF16_ROWS)]
            for hh in range(PEER_HEADS):
                cnt = jnp.broadcast_to(cnt8[hh:hh + 1], (BF16_ROWS, LANES)).astype(BF16)
                fa = jnp.broadcast_to(fa8[hh:hh + 1], (BF16_ROWS, LANES)).astype(BF16)
                for sub in range(re // BF16_ROWS):
                    ks = slice(k0 + sub * BF16_ROWS, k0 + (sub + 1) * BF16_ROWS)
                    sel = r2_ref[0, hh, ks, lanes] < cnt
                    w[sub] = w[sub] + jnp.where(sel, b_ref[0, hh, ks, lanes], 0) * fa
            for sub in range(re // BF16_ROWS):
                e0 = r * re + sub * BF16_ROWS
                w_scr[cur, e0:e0 + BF16_ROWS, lanes] = w[sub]
        er = slice(r * re, (r + 1) * re)
        p_scr[prv, er, :] = w_scr[prv, er, :] * _gelu_tanh(act_scr[prv, er, :]).astype(BF16)

    @pl.when(j < nt)
    def _():
        act_scr[cur] = _dot(u_ref[...], ht_ref[0])


PEER_TE = 512


def _peer_experts(ht, u, vt, cnt, fa, r2, fb, tm=512):
    bsz, d, l = ht.shape
    ne = u.shape[0]
    nh, nk = PEER_HEADS, PEER_N_KEYS
    te = PEER_TE
    nt = ne // te
    tab_spec = pl.BlockSpec((1, nh, nk, tm), lambda b, i, j: (b, 0, 0, i),
                            pipeline_mode=pl.Buffered(1))
    tab_a_spec = pl.BlockSpec((1, nk, nh, tm), lambda b, i, j: (b, 0, 0, i),
                              pipeline_mode=pl.Buffered(1))
    return pl.pallas_call(
        functools.partial(_expert_kernel, te=te, tm=tm),
        grid=(bsz, l // tm, nt + 2),
        in_specs=[pl.BlockSpec((1, d, tm), lambda b, i, j: (b, 0, i), pipeline_mode=pl.Buffered(1)),
                  pl.BlockSpec((te, d), lambda b, i, j: (jnp.minimum(j, nt - 1), 0)),
                  pl.BlockSpec((1, d, te), lambda b, i, j: (jnp.maximum(j - 2, 0), 0, 0)),
                  tab_a_spec, tab_a_spec, tab_spec, tab_spec],
        out_specs=pl.BlockSpec((1, d, tm), lambda b, i, j: (b, 0, i)),
        out_shape=jax.ShapeDtypeStruct((bsz, d, l), F32),
        scratch_shapes=[pltpu.VMEM((2, te, tm), F32), pltpu.VMEM((2, te, tm), BF16),
                        pltpu.VMEM((2, te, tm), BF16)],
        compiler_params=_cparams(("parallel", "parallel", "arbitrary")),
        name="peer_experts",
    )(ht, u, vt, cnt, fa, r2, fb)


def _final_kernel(x_ref, pt_ref, g2_ref, fg_ref, o_ref):
    x = x_ref[0] + g2_ref[0] * pt_ref[0].T
    ms = jnp.mean(x * x, axis=-1, keepdims=True)
    o_ref[0] = (x * lax.rsqrt(ms + EPS)) * fg_ref[...]


def _final(x, peer_t, g2, fg, tm=256):
    bsz, l, d = x.shape
    return pl.pallas_call(
        _final_kernel,
        grid=(bsz, l // tm),
        in_specs=[pl.BlockSpec((1, tm, d), lambda b, i: (b, i, 0)),
                  pl.BlockSpec((1, d, tm), lambda b, i: (b, 0, i)),
                  pl.BlockSpec((1, 1, d), lambda b, i: (b, 0, 0)),
                  pl.BlockSpec((1, d), lambda b, i: (0, 0))],
        out_specs=pl.BlockSpec((1, tm, d), lambda b, i: (b, i, 0)),
        out_shape=jax.ShapeDtypeStruct((bsz, l, d), F32),
        compiler_params=_cparams(("parallel", "parallel")),
        name="final_norm",
    )(x, peer_t, g2, fg)


def _prepare_weights(w_in, gate_b, na_rpb, w_out, peer_wq, peer_k1, peer_k2, peer_u, peer_v):
    mw = MLSTM_HEADS * MLSTM_HEAD_DIM
    n_gates = 4 * MLSTM_HEADS
    w_main = jnp.concatenate([w_in[:, :4 * mw], w_in[:, 4 * mw + n_gates:]], axis=1).astype(BF16)
    w_gate = jnp.pad(w_in[:, 4 * mw:4 * mw + n_gates], ((0, 0), (0, LANES - n_gates))).astype(BF16)
    gb = gate_b.reshape(4, MLSTM_HEADS).T
    return dict(
        w_main=w_main, w_gate=w_gate,
        gate_bc=gb[:, None, :], gate_br=gb[:, :, None],
        na_bias=_na_bias_table(na_rpb),
        w_out=w_out.astype(BF16),
        wq_t=peer_wq.T.astype(BF16),
        k1=peer_k1, k2=peer_k2,
        u=peer_u.astype(BF16),
        vt=peer_v.astype(BF16).reshape(-1, PEER_TE, peer_v.shape[1]).transpose(0, 2, 1))


def _trunk(x, mod, norm1_g, mlstm_norm_g, norm2_g, final_g, wts):
    bsz, l, d = x.shape
    sh1, sc1, g1, sh2, sc2, g2 = [m[:, None, :] for m in jnp.split(mod, 6, axis=-1)]
    mw = MLSTM_HEADS * MLSTM_HEAD_DIM
    proj, gates = _inproj(x, norm1_g, sc1, sh1, wts["w_main"], wts["w_gate"])
    g4 = gates[:, :, :4 * MLSTM_HEADS].reshape(bsz, l, 4, MLSTM_HEADS)
    gc = g4.transpose(0, 3, 1, 2)
    gr = g4.transpose(0, 3, 2, 1)
    t = min(256, l)
    h_bwd = _mlstm_direction(proj, gc, gr, wts["gate_bc"], wts["gate_br"], True, t=t)
    hm = _mlstm_direction(proj, gc, gr, wts["gate_bc"], wts["gate_br"], False,
                          h_bwd=h_bwd, norm_g=mlstm_norm_g, t=t)
    hn = _neighbourhood_attention(proj, wts["na_bias"], 4 * mw // NA_HEAD_DIM)
    x1 = _outproj(hm, hn, wts["w_out"], x, g1)
    h2t, cnt, fa, r2, fb = _peer_route(x1, norm2_g, sc2, sh2, wts["wq_t"], wts["k1"], wts["k2"])
    peer_t = _peer_experts(h2t, wts["u"], wts["vt"], cnt.transpose(0, 2, 1, 3),
                           fa.transpose(0, 2, 1, 3), r2, fb)
    return _final(x1, peer_t, g2, final_g)


def kernel(x_prompt, x_sample, c_prompt, c_sample, ada_w, ada_b, norm1_g, w_in, gate_b, mlstm_norm_g, na_rpb, w_out, norm2_g, peer_wq, peer_k1, peer_k2, peer_u, peer_v, final_g):
    assert ada_w.shape[0] == 1, "single-layer trunk"
    nb_p = c_prompt.shape[0]
    nb_s = c_sample.shape[0]
    rows = -(-(nb_p + nb_s) // 8) * 8
    c_all = jnp.concatenate([c_prompt, c_sample], axis=0)
    c_all = jnp.pad(c_all, ((0, rows - nb_p - nb_s), (0, 0)))
    mod = _modulation(c_all, ada_w[0], ada_b)
    wts = _prepare_weights(w_in[0], gate_b[0], na_rpb[0], w_out[0], peer_wq[0], peer_k1[0],
                           peer_k2[0], peer_u[0], peer_v[0])
    fg = final_g[None, :]
    y_prompt = _trunk(x_prompt, mod[:nb_p], norm1_g, mlstm_norm_g, norm2_g, fg, wts)
    y_sample = _trunk(x_sample, mod[nb_p:nb_p + nb_s], norm1_g, mlstm_norm_g, norm2_g, fg, wts)
    return (y_prompt, y_sample)
```

```python
import functools

import jax
import jax.numpy as jnp
from jax import lax
from jax.experimental import pallas as pl
from jax.experimental.pallas import tpu as pltpu

F32 = jnp.float32
BF16 = jnp.bfloat16
FP8 = jnp.float8_e4m3fn
FP8_MAX = 448.0
FP8_TINY = 1e-30

GRID_W = 64
MLSTM_HEADS = 8
MLSTM_HEAD_DIM = 256
NA_HEADS = 16
NA_HEAD_DIM = 128
NA_WIN_ROWS = 8
NA_WIN_COLS = 16
PEER_HEADS = 8
PEER_HALF = 128
PEER_N_KEYS = 128
PEER_TOPK = 16
EPS = 1e-6

LANES = 128
BF16_ROWS = 16
VMEM_LIMIT = 56 * 1024 * 1024

NEG_INF = float("-inf")


def _cparams(sem):
    return pltpu.CompilerParams(dimension_semantics=sem, vmem_limit_bytes=VMEM_LIMIT)


def _dot(a, b):
    return jnp.dot(a, b, preferred_element_type=F32)


def _dot_nt(a, b):
    return lax.dot_general(a, b, (((1,), (1,)), ((), ())), preferred_element_type=F32)


def _mod_kernel(c_ref, w_ref, b_ref, o_ref):
    c = c_ref[...]
    s = c * jax.nn.sigmoid(c)
    hi = s.astype(BF16)
    lo = (s - hi.astype(F32)).astype(BF16)
    w = w_ref[...].astype(BF16)
    o_ref[...] = _dot(hi, w) + _dot(lo, w) + b_ref[...]


def _modulation(c, w, b, tn=512):
    r, d = c.shape
    n = w.shape[1]
    return pl.pallas_call(
        _mod_kernel,
        grid=(n // tn,),
        in_specs=[pl.BlockSpec((r, d), lambda j: (0, 0)),
                  pl.BlockSpec((d, tn), lambda j: (0, j)),
                  pl.BlockSpec((1, tn), lambda j: (0, j))],
        out_specs=pl.BlockSpec((r, tn), lambda j: (0, j)),
        out_shape=jax.ShapeDtypeStruct((r, n), F32),
        compiler_params=_cparams(("parallel",)),
        name="modulation",
    )(c, w, b)


def _ada_norm(x, g, sc, sh):
    ms = jnp.mean(x * x, axis=-1, keepdims=True)
    y = x * lax.rsqrt(ms + EPS)
    return (y * g) * (1.0 + sc) + sh


def _inproj_kernel(x_ref, g_ref, sc_ref, sh_ref, w_ref, wg_ref, o_ref, og_ref, h_scr):
    @pl.when(pl.program_id(2) == 0)
    def _():
        h = _ada_norm(x_ref[0], g_ref[...], sc_ref[0], sh_ref[0]).astype(BF16)
        h_scr[...] = h
        og_ref[0] = _dot(h, wg_ref[...])

    o_ref[0] = _dot(h_scr[...], w_ref[...]).astype(BF16)


def _inproj(x, g, sc, sh, w, wg, tm=512, tn=1024):
    bsz, l, d = x.shape
    n = w.shape[1]
    ng = wg.shape[1]
    return pl.pallas_call(
        _inproj_kernel,
        grid=(bsz, l // tm, n // tn),
        in_specs=[pl.BlockSpec((1, tm, d), lambda b, i, j: (b, i, 0)),
                  pl.BlockSpec((1, d), lambda b, i, j: (0, 0)),
                  pl.BlockSpec((1, 1, d), lambda b, i, j: (b, 0, 0)),
                  pl.BlockSpec((1, 1, d), lambda b, i, j: (b, 0, 0)),
                  pl.BlockSpec((d, tn), lambda b, i, j: (0, j)),
                  pl.BlockSpec((d, ng), lambda b, i, j: (0, 0))],
        out_specs=[pl.BlockSpec((1, tm, tn), lambda b, i, j: (b, i, j)),
                   pl.BlockSpec((1, tm, ng), lambda b, i, j: (b, i, 0))],
        out_shape=[jax.ShapeDtypeStruct((bsz, l, n), BF16),
                   jax.ShapeDtypeStruct((bsz, l, ng), F32)],
        scratch_shapes=[pltpu.VMEM((tm, d), BF16)],
        compiler_params=_cparams(("parallel", "parallel", "arbitrary")),
        name="inproj",
    )(x, g, sc, sh, w, wg)


def _log_sigmoid(x):
    return jnp.minimum(x, 0.0) - jnp.log(1.0 + jnp.exp(-jnp.abs(x)))


def _mlstm_chunk(q, k, v, i_col, f_col, i_row, f_row, c_st, n_st, m_st, reverse):
    t = q.shape[0]
    kscale = MLSTM_HEAD_DIM ** -0.5
    r_i = lax.broadcasted_iota(jnp.int32, (t, t), 0)
    c_i = lax.broadcasted_iota(jnp.int32, (t, t), 1)
    seen = (c_i >= r_i) if reverse else (c_i <= r_i)
    seen_t = (r_i >= c_i) if reverse else (r_i <= c_i)
    b_col = jnp.sum(jnp.where(seen, f_row, 0.0), axis=1, keepdims=True)
    b_row = jnp.sum(jnp.where(seen_t, f_col, 0.0), axis=0, keepdims=True)
    d = jnp.where(seen, b_col - b_row + i_row, NEG_INF)
    inter = b_col + m_st
    m_t = jnp.maximum(inter, jnp.max(d, axis=1, keepdims=True))
    w_inter = jnp.exp(inter - m_t)
    p = jnp.exp(d - m_t) * (_dot_nt(q, k) * kscale)
    num = w_inter * _dot(q, c_st.astype(BF16)) + _dot(p.astype(BF16), v)
    den = (w_inter * jnp.sum(q.astype(F32) * n_st, axis=1, keepdims=True)
           + jnp.sum(p, axis=1, keepdims=True))
    h = num / jnp.maximum(jnp.abs(den), jnp.exp(-m_t))
    b_last = b_col[0:1] if reverse else b_col[t - 1:t]
    dec = b_last - b_col + i_col
    m_new = jnp.maximum(b_last + m_st, jnp.max(dec, axis=0, keepdims=True))
    w_old = jnp.exp(b_last + m_st - m_new)
    w_s = jnp.exp(dec - m_new)
    kw = k.astype(F32) * (w_s * kscale)
    c_new = w_old * c_st + _dot(kw.T.astype(BF16), v)
    n_new = w_old * n_st + jnp.sum(kw, axis=0, keepdims=True)
    return h, c_new, n_new, m_new


def _mlstm_kernel(*refs, reverse, hb):
    if reverse:
        q_ref, k_ref, v_ref, gc_ref, gr_ref, bc_ref, br_ref, o_ref, c_scr, n_scr, m_scr = refs
    else:
        (q_ref, k_ref, v_ref, gc_ref, gr_ref, bc_ref, br_ref, hb_ref, mo_ref, ng_ref,
         o_ref, c_scr, n_scr, m_scr) = refs

    @pl.when(pl.program_id(2) == 0)
    def _():
        c_scr[...] = jnp.zeros_like(c_scr)
        n_scr[...] = jnp.zeros_like(n_scr)
        m_scr[...] = jnp.zeros_like(m_scr)

    dh = MLSTM_HEAD_DIM
    gi = 2 if reverse else 0
    for hh in range(hb):
        cols = slice(hh * dh, (hh + 1) * dh)
        gc = gc_ref[0, hh] + bc_ref[hh]
        gr = gr_ref[0, hh] + br_ref[hh]
        h, c_new, n_new, m_new = _mlstm_chunk(
            q_ref[0, :, cols], k_ref[0, :, cols], v_ref[0, :, cols],
            gc[:, gi:gi + 1], _log_sigmoid(gc[:, gi + 1:gi + 2]),
            gr[gi:gi + 1, :], _log_sigmoid(gr[gi + 1:gi + 2, :]),
            c_scr[hh], n_scr[hh], m_scr[hh], reverse)
        c_scr[hh] = c_new
        n_scr[hh] = n_new
        m_scr[hh] = m_new
        if reverse:
            o_ref[0, :, cols] = h
        else:
            ht = h + hb_ref[0, :, cols]
            ht = ht * lax.rsqrt(jnp.mean(ht * ht, axis=-1, keepdims=True) + EPS)
            ht = ht * ng_ref[:, cols]
            o_ref[0, :, cols] = (ht * jax.nn.sigmoid(mo_ref[0, :, cols].astype(F32))).astype(BF16)


def _mlstm_direction(pm, gc, gr, bc, br, reverse, h_bwd=None, norm_g=None, t=256, hb=2):
    bsz, l, _ = pm.shape
    nh = MLSTM_HEADS
    dh = MLSTM_HEAD_DIM
    w = nh * dh
    nc = l // t
    ng = nh // hb
    wb = hb * dh

    def ci(c):
        return (nc - 1 - c) if reverse else c

    in_specs = [pl.BlockSpec((1, t, wb), lambda b, g, c: (b, ci(c), g)),
                pl.BlockSpec((1, t, wb), lambda b, g, c: (b, ci(c), ng + g)),
                pl.BlockSpec((1, t, wb), lambda b, g, c: (b, ci(c), 2 * ng + g)),
                pl.BlockSpec((1, hb, t, 4), lambda b, g, c: (b, g, ci(c), 0)),
                pl.BlockSpec((1, hb, 4, t), lambda b, g, c: (b, g, 0, ci(c))),
                pl.BlockSpec((hb, 1, 4), lambda b, g, c: (g, 0, 0)),
                pl.BlockSpec((hb, 4, 1), lambda b, g, c: (g, 0, 0))]
    args = [pm, pm, pm, gc, gr, bc, br]
    if not reverse:
        in_specs += [pl.BlockSpec((1, t, wb), lambda b, g, c: (b, c, g)),
                     pl.BlockSpec((1, t, wb), lambda b, g, c: (b, c, 3 * ng + g)),
                     pl.BlockSpec((1, wb), lambda b, g, c: (0, g))]
        args += [h_bwd, pm, norm_g]
    return pl.pallas_call(
        functools.partial(_mlstm_kernel, reverse=reverse, hb=hb),
        grid=(bsz, ng, nc),
        in_specs=in_specs,
        out_specs=pl.BlockSpec((1, t, wb), lambda b, g, c: (b, ci(c), g)),
        out_shape=jax.ShapeDtypeStruct((bsz, l, w), F32 if reverse else BF16),
        scratch_shapes=[pltpu.VMEM((hb, dh, dh), F32),
                        pltpu.VMEM((hb, 1, dh), F32),
                        pltpu.VMEM((hb, 1, 1), F32)],
        compiler_params=_cparams(("parallel", "parallel", "arbitrary")),
        name="mlstm_bwd" if reverse else "mlstm_fwd",
    )(*args)


def _na_bias_table(rpb):
    c = jnp.arange(GRID_W)
    cs = jnp.clip(c - NA_WIN_COLS // 2, 0, GRID_W - NA_WIN_COLS)
    col_ok = (c[None, :] >= cs[:, None]) & (c[None, :] < cs[:, None] + NA_WIN_COLS)
    dc = jnp.clip(c[None, :] - c[:, None], -(NA_WIN_COLS - 1), NA_WIN_COLS - 1) + (NA_WIN_COLS - 1)
    onehot = (dc[None] == jnp.arange(2 * NA_WIN_COLS - 1)[:, None, None]).astype(F32)
    t1 = jnp.einsum("hdc,cqk->hdqk", rpb.astype(F32), onehot, precision=lax.Precision.HIGHEST)
    t1 = jnp.where(col_ok[None, None], t1, NEG_INF)
    fill = jnp.full_like(t1[:, :1], NEG_INF)
    lo = jnp.concatenate([fill, t1], axis=1)
    hi = jnp.concatenate([t1, fill], axis=1)
    return jnp.concatenate([lo, hi], axis=-1)


NA_RBLK = 8
NA_KROWS = 2 * NA_WIN_ROWS


def _na_block(q_ref, kw, vw, tp_ref, o_ref, p_scr, shift, starts):
    w2 = 2 * GRID_W
    scale = NA_HEAD_DIM ** -0.5
    s = _dot_nt(q_ref[0], kw) * scale
    lane = lax.broadcasted_iota(jnp.int32, (GRID_W, w2), 1)
    inv_l = []
    for jr in range(NA_RBLK):
        rows = slice(jr * GRID_W, (jr + 1) * GRID_W)
        st = starts[jr]
        blocks = {}
        for pr in range(NA_KROWS // 2):
            ok0 = st <= 2 * pr < st + NA_WIN_ROWS
            ok1 = st <= 2 * pr + 1 < st + NA_WIN_ROWS
            if not (ok0 or ok1):
                p_scr[rows, pr * w2:(pr + 1) * w2] = jnp.zeros((GRID_W, w2), BF16)
                continue
            d0 = shift + 2 * pr - jr + (NA_WIN_ROWS - 1)
            x = s[rows, pr * w2:(pr + 1) * w2] + tp_ref[0, d0 + 1]
            if not ok0:
                x = jnp.where(lane >= GRID_W, x, NEG_INF)
            if not ok1:
                x = jnp.where(lane < GRID_W, x, NEG_INF)
            blocks[pr] = x
        m = functools.reduce(jnp.maximum, blocks.values())
        m = jnp.max(m, axis=-1, keepdims=True)
        tot = None
        for pr, x in blocks.items():
            p = jnp.exp(x - m)
            tot = p if tot is None else tot + p
            p_scr[rows, pr * w2:(pr + 1) * w2] = p.astype(BF16)
        inv_l.append(1.0 / jnp.sum(tot, axis=-1, keepdims=True))
    o = _dot(p_scr[...], vw)
    for jr in range(NA_RBLK):
        rows = slice(jr * GRID_W, (jr + 1) * GRID_W)
        o_ref[0, rows, :] = (o[rows] * inv_l[jr]).astype(BF16)


def _na_kernel(q_ref, k_ref, v_ref, tp_ref, o_ref, p_scr, *, rows):
    i = pl.program_id(2)
    nblk = rows // NA_RBLK
    half = NA_WIN_ROWS // 2
    kwin = NA_KROWS * GRID_W

    def run(base, shift, starts):
        kpos = base * GRID_W
        if not isinstance(kpos, int):
            kpos = pl.multiple_of(kpos, GRID_W)
        _na_block(q_ref, k_ref[0, pl.ds(kpos, kwin), :], v_ref[0, pl.ds(kpos, kwin), :],
                  tp_ref, o_ref, p_scr, shift, starts)

    @pl.when(i == 0)
    def _():
        run(0, 0, [max(jr - half, 0) for jr in range(NA_RBLK)])

    @pl.when(i == nblk - 1)
    def _():
        run(rows - NA_KROWS, NA_RBLK - NA_KROWS,
            [min(jr + half, NA_KROWS - NA_WIN_ROWS) for jr in range(NA_RBLK)])

    @pl.when(jnp.logical_and(i > 0, i < nblk - 1))
    def _():
        run(i * NA_RBLK - half, -half, list(range(NA_RBLK)))


def _neighbourhood_attention(pn, bias_tab, col0):
    bsz, l, _ = pn.shape
    nh, dh = NA_HEADS, NA_HEAD_DIM
    rows = l // GRID_W
    assert rows >= NA_KROWS and rows % NA_RBLK == 0
    tq = NA_RBLK * GRID_W
    return pl.pallas_call(
        functools.partial(_na_kernel, rows=rows),
        grid=(bsz, nh, rows // NA_RBLK),
        in_specs=[pl.BlockSpec((1, tq, dh), lambda b, h, i: (b, i, col0 + h)),
                  pl.BlockSpec((1, l, dh), lambda b, h, i: (b, 0, col0 + nh + h)),
                  pl.BlockSpec((1, l, dh), lambda b, h, i: (b, 0, col0 + 2 * nh + h)),
                  pl.BlockSpec((1, 2 * NA_WIN_ROWS, GRID_W, 2 * GRID_W), lambda b, h, i: (h, 0, 0, 0))],
        out_specs=pl.BlockSpec((1, tq, dh), lambda b, h, i: (b, i, h)),
        out_shape=jax.ShapeDtypeStruct((bsz, l, nh * dh), BF16),
        scratch_shapes=[pltpu.VMEM((tq, NA_KROWS * GRID_W), BF16)],
        compiler_params=_cparams(("parallel", "parallel", "arbitrary")),
        name="natten",
    )(pn, pn, pn, bias_tab)


def _outproj_kernel(hm_ref, hn_ref, w1_ref, w2_ref, x_ref, g_ref, o_ref):
    acc = _dot(hm_ref[0], w1_ref[...]) + _dot(hn_ref[0], w2_ref[...])
    o_ref[0] = x_ref[0] + g_ref[0] * acc


def _outproj(hm, hn, w, x, g, tm=512, tn=1024):
    bsz, l, d = x.shape
    kh = hm.shape[2]
    return pl.pallas_call(
        _outproj_kernel,
        grid=(bsz, l // tm, d // tn),
        in_specs=[pl.BlockSpec((1, tm, kh), lambda b, i, j: (b, i, 0)),
                  pl.BlockSpec((1, tm, kh), lambda b, i, j: (b, i, 0)),
                  pl.BlockSpec((kh, tn), lambda b, i, j: (0, j)),
                  pl.BlockSpec((kh, tn), lambda b, i, j: (1, j)),
                  pl.BlockSpec((1, tm, tn), lambda b, i, j: (b, i, j)),
                  pl.BlockSpec((1, 1, tn), lambda b, i, j: (b, 0, j))],
        out_specs=pl.BlockSpec((1, tm, tn), lambda b, i, j: (b, i, j)),
        out_shape=jax.ShapeDtypeStruct((bsz, l, d), F32),
        compiler_params=_cparams(("parallel", "parallel", "parallel")),
        name="outproj",
    )(hm, hn, w, w, x, g)


def _top_values_axis0(chunks, k, group):
    vals = [[] for _ in chunks]
    for g0 in range(0, len(chunks), group):
        ids = range(g0, min(g0 + group, len(chunks)))
        state = {c: chunks[c] for c in ids}
        for _ in range(k):
            for c in ids:
                mx = jnp.max(state[c], axis=0, keepdims=True)
                vals[c].append(mx)
                state[c] = jnp.where(state[c] == mx, NEG_INF, state[c])
    return vals


def _count_reaching(rows, x, strict):
    cnt = jnp.zeros_like(x)
    for row in rows:
        cnt = cnt + jnp.where((row > x) if strict else (row >= x), 1.0, 0.0)
    return cnt


_CAND_PAIRS = [(a, b) for a in range(PEER_TOPK + 1) for b in range(PEER_TOPK + 1)
               if (a + 1) * (b + 1) <= PEER_TOPK + 1]
_CAND_ROWS = -(-len(_CAND_PAIRS) // 8) * 8


def _split2(x):
    hi = x.astype(BF16)
    return hi, (x - hi.astype(F32)).astype(BF16)


def _dot_f32(a, b):
    a0, a1 = _split2(a)
    b0, b1 = _split2(b)
    return _dot(a0, b0) + (_dot(a0, b1) + _dot(a1, b0))


def _route_kernel(x_ref, g_ref, sc_ref, sh_ref, wq_ref, k1_ref, k2_ref,
                  hq_ref, st_ref, cnt_ref, a_ref, r2_ref, b_ref, ht_scr, cand_scr):
    @pl.when(pl.program_id(2) == 0)
    def _():
        h = _ada_norm(x_ref[0], g_ref[...], sc_ref[0], sh_ref[0])
        ht = h.T
        ht_scr[...] = ht.astype(BF16)
        st = jnp.maximum(jnp.max(jnp.abs(ht), axis=0, keepdims=True), FP8_TINY) * (1.0 / FP8_MAX)
        st_ref[0] = st
        hq_ref[0] = (ht / st).astype(FP8)

    qt = _dot(wq_ref[...], ht_scr[...])
    s1 = _dot_f32(k1_ref[0], qt[:PEER_HALF])
    s2 = _dot_f32(k2_ref[0], qt[PEER_HALF:])
    nchunk = s1.shape[1] // LANES
    lanes = [slice(c * LANES, (c + 1) * LANES) for c in range(nchunk)]
    s1c = [s1[:, ln] for ln in lanes]
    s2c = [s2[:, ln] for ln in lanes]
    ktop = PEER_TOPK + 1
    v1 = _top_values_axis0(s1c, ktop, group=2)
    v2 = _top_values_axis0(s2c, ktop, group=2)
    cand_scr[...] = jnp.full(cand_scr.shape, NEG_INF, F32)
    for c in range(nchunk):
        for i, (ka, kb) in enumerate(_CAND_PAIRS):
            cand_scr[i:i + 1, lanes[c]] = v1[c][ka] + v2[c][kb]
    sc = _top_values_axis0([cand_scr[:, ln] for ln in lanes], ktop, group=nchunk)
    for c in range(nchunk):
        z = jnp.ones_like(sc[c][0])
        for kk in range(1, PEER_TOPK):
            z = z + jnp.exp(sc[c][kk] - sc[c][0])
        thr = 0.5 * (sc[c][PEER_TOPK - 1] + sc[c][PEER_TOPK]) - s1c[c]
        cnt_ref[0, 0, :, lanes[c]] = _count_reaching(v2[c], thr, strict=False)
        r2_ref[0, 0, :, lanes[c]] = _count_reaching(v2[c], s2c[c], strict=True).astype(BF16)
        a_ref[0, 0, :, lanes[c]] = jnp.exp(s1c[c] - v1[c][0]) / z
        b_ref[0, 0, :, lanes[c]] = jnp.exp(s2c[c] - v2[c][0]).astype(BF16)


def _peer_route(x, g, sc, sh, wq_t, k1, k2, tm=512):
    bsz, l, d = x.shape
    nh = PEER_HEADS
    qd = 2 * PEER_HALF
    nk = PEER_N_KEYS
    tab = jax.ShapeDtypeStruct((bsz, nh, nk, l), F32)
    tab16 = jax.ShapeDtypeStruct((bsz, nh, nk, l), BF16)
    tab_spec = pl.BlockSpec((1, 1, nk, tm), lambda b, i, h: (b, h, 0, i))
    return pl.pallas_call(
        _route_kernel,
        grid=(bsz, l // tm, nh),
        in_specs=[pl.BlockSpec((1, tm, d), lambda b, i, h: (b, i, 0)),
                  pl.BlockSpec((1, d), lambda b, i, h: (0, 0)),
                  pl.BlockSpec((1, 1, d), lambda b, i, h: (b, 0, 0)),
                  pl.BlockSpec((1, 1, d), lambda b, i, h: (b, 0, 0)),
                  pl.BlockSpec((qd, d), lambda b, i, h: (h, 0)),
                  pl.BlockSpec((1, nk, PEER_HALF), lambda b, i, h: (h, 0, 0)),
                  pl.BlockSpec((1, nk, PEER_HALF), lambda b, i, h: (h, 0, 0))],
        out_specs=[pl.BlockSpec((1, d, tm), lambda b, i, h: (b, 0, i)),
                   pl.BlockSpec((1, 1, tm), lambda b, i, h: (b, 0, i)),
                   tab_spec, tab_spec, tab_spec, tab_spec],
        out_shape=[jax.ShapeDtypeStruct((bsz, d, l), FP8), jax.ShapeDtypeStruct((bsz, 1, l), F32),
                   tab, tab, tab16, tab16],
        scratch_shapes=[pltpu.VMEM((d, tm), BF16), pltpu.VMEM((_CAND_ROWS, tm), F32)],
        compiler_params=_cparams(("parallel", "parallel", "arbitrary")),
        name="peer_route",
    )(x, g, sc, sh, wq_t, k1, k2)


def _gelu_tanh(x):
    return 0.5 * x * (1.0 + jnp.tanh(0.7978845608028654 * (x + 0.044715 * (x * x * x))))


EXPERT_ROUNDS = 8


def _expert_kernel(hq_ref, st_ref, uq_ref, su_ref, vt_ref, cnt_ref, a_ref, r2_ref, b_ref, o_ref,
                   act_scr, w_scr, p_scr, *, te, tm):
    j = pl.program_id(2)
    nt = pl.num_programs(2) - 2
    nk = PEER_N_KEYS
    cur = j % 2
    prv = 1 - cur
    jt = jnp.minimum(j, nt - 1)
    d = vt_ref.shape[1]

    @pl.when(j == 0)
    def _():
        o_ref[...] = jnp.zeros_like(o_ref)
        act_scr[...] = jnp.zeros_like(act_scr)
        w_scr[...] = jnp.zeros_like(w_scr)
        p_scr[...] = jnp.zeros_like(p_scr)

    nr = EXPERT_ROUNDS
    rd, re = d // nr, te // nr
    mh, nh = te // 2, tm // 2
    assert nk % re == 0 and nr == 8
    for r in range(nr):
        rows = slice(r * rd, (r + 1) * rd)
        o_ref[0, rows, :] += _dot(vt_ref[0, rows, :], p_scr[cur])
        a_idx = jt * (te // nk) + (r * re) // nk
        k0 = (r * re) % nk
        for tc in range(tm // LANES):
            lanes = slice(tc * LANES, (tc + 1) * LANES)
            cnt8 = cnt_ref[0, a_idx, :, lanes]
            fa8 = a_ref[0, a_idx, :, lanes]
            w = [jnp.zeros((BF16_ROWS, LANES), BF16) for _ in range(re // BF16_ROWS)]
            for hh in range(PEER_HEADS):
                cnt = jnp.broadcast_to(cnt8[hh:hh + 1], (BF16_ROWS, LANES)).astype(BF16)
                fa = jnp.broadcast_to(fa8[hh:hh + 1], (BF16_ROWS, LANES)).astype(BF16)
                for sub in range(re // BF16_ROWS):
                    ks = slice(k0 + sub * BF16_ROWS, k0 + (sub + 1) * BF16_ROWS)
                    sel = r2_ref[0, hh, ks, lanes] < cnt
                    w[sub] = w[sub] + jnp.where(sel, b_ref[0, hh, ks, lanes], 0) * fa
            for sub in range(re // BF16_ROWS):
                e0 = r * re + sub * BF16_ROWS
                w_scr[cur, e0:e0 + BF16_ROWS, lanes] = w[sub]
        er = slice(r * re, (r + 1) * re)
        p_scr[prv, er, :] = w_scr[prv, er, :] * _gelu_tanh(act_scr[prv, er, :]).astype(BF16)
        if r % 2 == 0:
            qm, qn = (r // 2) // 2, (r // 2) % 2
            er = slice(qm * mh, (qm + 1) * mh)
            acc = _dot(uq_ref[er, :], hq_ref[0, :, qn * nh:(qn + 1) * nh])
            for lc in range(nh // LANES):
                lanes = slice(qn * nh + lc * LANES, qn * nh + (lc + 1) * LANES)
                act_scr[cur, er, lanes] = (acc[:, lc * LANES:(lc + 1) * LANES] * su_ref[er, :]
                                           * st_ref[0, :, lanes])


PEER_TE = 512


def _peer_experts(hq, st, uq, su, vt, cnt, fa, r2, fb, tm=512):
    bsz, d, l = hq.shape
    ne = uq.shape[0]
    nh, nk = PEER_HEADS, PEER_N_KEYS
    te = PEER_TE
    nt = ne // te
    tab_spec = pl.BlockSpec((1, nh, nk, tm), lambda b, i, j: (b, 0, 0, i),
                            pipeline_mode=pl.Buffered(1))
    tab_a_spec = pl.BlockSpec((1, nk, nh, tm), lambda b, i, j: (b, 0, 0, i),
                              pipeline_mode=pl.Buffered(1))
    return pl.pallas_call(
        functools.partial(_expert_kernel, te=te, tm=tm),
        grid=(bsz, l // tm, nt + 2),
        in_specs=[pl.BlockSpec((1, d, tm), lambda b, i, j: (b, 0, i), pipeline_mode=pl.Buffered(1)),
                  pl.BlockSpec((1, 1, tm), lambda b, i, j: (b, 0, i)),
                  pl.BlockSpec((te, d), lambda b, i, j: (jnp.minimum(j, nt - 1), 0)),
                  pl.BlockSpec((te, LANES), lambda b, i, j: (jnp.minimum(j, nt - 1), 0)),
                  pl.BlockSpec((1, d, te), lambda b, i, j: (jnp.maximum(j - 2, 0), 0, 0)),
                  tab_a_spec, tab_a_spec, tab_spec, tab_spec],
        out_specs=pl.BlockSpec((1, d, tm), lambda b, i, j: (b, 0, i)),
        out_shape=jax.ShapeDtypeStruct((bsz, d, l), F32),
        scratch_shapes=[pltpu.VMEM((2, te, tm), F32), pltpu.VMEM((2, te, tm), BF16),
                        pltpu.VMEM((2, te, tm), BF16)],
        compiler_params=_cparams(("parallel", "parallel", "arbitrary")),
        name="peer_experts",
    )(hq, st, uq, su, vt, cnt, fa, r2, fb)


def _final_kernel(x_ref, pt_ref, g2_ref, fg_ref, o_ref):
    x = x_ref[0] + g2_ref[0] * pt_ref[0].T
    ms = jnp.mean(x * x, axis=-1, keepdims=True)
    o_ref[0] = (x * lax.rsqrt(ms + EPS)) * fg_ref[...]


def _final(x, peer_t, g2, fg, tm=256):
    bsz, l, d = x.shape
    return pl.pallas_call(
        _final_kernel,
        grid=(bsz, l // tm),
        in_specs=[pl.BlockSpec((1, tm, d), lambda b, i: (b, i, 0)),
                  pl.BlockSpec((1, d, tm), lambda b, i: (b, 0, i)),
                  pl.BlockSpec((1, 1, d), lambda b, i: (b, 0, 0)),
                  pl.BlockSpec((1, d), lambda b, i: (0, 0))],
        out_specs=pl.BlockSpec((1, tm, d), lambda b, i: (b, i, 0)),
        out_shape=jax.ShapeDtypeStruct((bsz, l, d), F32),
        compiler_params=_cparams(("parallel", "parallel")),
        name="final_norm",
    )(x, peer_t, g2, fg)


def _prepare_weights(w_in, gate_b, na_rpb, w_out, peer_wq, peer_k1, peer_k2, peer_u, peer_v):
    mw = MLSTM_HEADS * MLSTM_HEAD_DIM
    n_gates = 4 * MLSTM_HEADS
    w_main = jnp.concatenate([w_in[:, :4 * mw], w_in[:, 4 * mw + n_gates:]], axis=1).astype(BF16)
    w_gate = jnp.pad(w_in[:, 4 * mw:4 * mw + n_gates], ((0, 0), (0, LANES - n_gates))).astype(BF16)
    gb = gate_b.reshape(4, MLSTM_HEADS).T
    su = jnp.maximum(jnp.max(jnp.abs(peer_u), axis=1, keepdims=True), FP8_TINY) * (1.0 / FP8_MAX)
    return dict(
        w_main=w_main, w_gate=w_gate,
        gate_bc=gb[:, None, :], gate_br=gb[:, :, None],
        na_bias=_na_bias_table(na_rpb),
        w_out=w_out.astype(BF16),
        wq_t=peer_wq.T.astype(BF16),
        k1=peer_k1, k2=peer_k2,
        uq=(peer_u / su).astype(FP8), su=jnp.broadcast_to(su, (su.shape[0], LANES)),
        vt=peer_v.astype(BF16).reshape(-1, PEER_TE, peer_v.shape[1]).transpose(0, 2, 1))


def _trunk(x, mod, norm1_g, mlstm_norm_g, norm2_g, final_g, wts):
    bsz, l, d = x.shape
    sh1, sc1, g1, sh2, sc2, g2 = [m[:, None, :] for m in jnp.split(mod, 6, axis=-1)]
    mw = MLSTM_HEADS * MLSTM_HEAD_DIM
    proj, gates = _inproj(x, norm1_g, sc1, sh1, wts["w_main"], wts["w_gate"])
    g4 = gates[:, :, :4 * MLSTM_HEADS].reshape(bsz, l, 4, MLSTM_HEADS)
    gc = g4.transpose(0, 3, 1, 2)
    gr = g4.transpose(0, 3, 2, 1)
    t = min(256, l)
    h_bwd = _mlstm_direction(proj, gc, gr, wts["gate_bc"], wts["gate_br"], True, t=t)
    hm = _mlstm_direction(proj, gc, gr, wts["gate_bc"], wts["gate_br"], False,
                          h_bwd=h_bwd, norm_g=mlstm_norm_g, t=t)
    hn = _neighbourhood_attention(proj, wts["na_bias"], 4 * mw // NA_HEAD_DIM)
    x1 = _outproj(hm, hn, wts["w_out"], x, g1)
    hq, st, cnt, fa, r2, fb = _peer_route(x1, norm2_g, sc2, sh2, wts["wq_t"], wts["k1"], wts["k2"])
    peer_t = _peer_experts(hq, st, wts["uq"], wts["su"], wts["vt"], cnt.transpose(0, 2, 1, 3),
                           fa.transpose(0, 2, 1, 3), r2, fb)
    return _final(x1, peer_t, g2, final_g)


def kernel(x_prompt, x_sample, c_prompt, c_sample, ada_w, ada_b, norm1_g, w_in, gate_b, mlstm_norm_g, na_rpb, w_out, norm2_g, peer_wq, peer_k1, peer_k2, peer_u, peer_v, final_g):
    assert ada_w.shape[0] == 1, "single-layer trunk"
    nb_p = c_prompt.shape[0]
    nb_s = c_sample.shape[0]
    rows = -(-(nb_p + nb_s) // 8) * 8
    c_all = jnp.concatenate([c_prompt, c_sample], axis=0)
    c_all = jnp.pad(c_all, ((0, rows - nb_p - nb_s), (0, 0)))
    mod = _modulation(c_all, ada_w[0], ada_b)
    wts = _prepare_weights(w_in[0], gate_b[0], na_rpb[0], w_out[0], peer_wq[0], peer_k1[0],
                           peer_k2[0], peer_u[0], peer_v[0])
    fg = final_g[None, :]
    y_prompt = _trunk(x_prompt, mod[:nb_p], norm1_g, mlstm_norm_g, norm2_g, fg, wts)
    y_sample = _trunk(x_sample, mod[nb_p:nb_p + nb_s], norm1_g, mlstm_norm_g, norm2_g, fg, wts)
    return (y_prompt, y_sample)
```

```python
import functools

import jax
import jax.numpy as jnp
from jax import lax
from jax.experimental import pallas as pl
from jax.experimental.pallas import tpu as pltpu

F32 = jnp.float32
BF16 = jnp.bfloat16
FP8 = jnp.float8_e4m3fn
FP8_MAX = 448.0
FP8_TINY = 1e-30

GRID_W = 64
MLSTM_HEADS = 8
MLSTM_HEAD_DIM = 256
NA_HEADS = 16
NA_HEAD_DIM = 128
NA_WIN_ROWS = 8
NA_WIN_COLS = 16
PEER_HEADS = 8
PEER_HALF = 128
PEER_N_KEYS = 128
PEER_TOPK = 16
EPS = 1e-6

LANES = 128
BF16_ROWS = 16
VMEM_LIMIT = 56 * 1024 * 1024

NEG_INF = float("-inf")


def _cparams(sem):
    return pltpu.CompilerParams(dimension_semantics=sem, vmem_limit_bytes=VMEM_LIMIT)


def _dot(a, b):
    return jnp.dot(a, b, preferred_element_type=F32)


def _dot_nt(a, b):
    return lax.dot_general(a, b, (((1,), (1,)), ((), ())), preferred_element_type=F32)


def _mod_kernel(c_ref, w_ref, b_ref, o_ref):
    c = c_ref[...]
    s = c * jax.nn.sigmoid(c)
    hi = s.astype(BF16)
    lo = (s - hi.astype(F32)).astype(BF16)
    w = w_ref[...].astype(BF16)
    o_ref[...] = _dot(hi, w) + _dot(lo, w) + b_ref[...]


def _modulation(c, w, b, tn=512):
    r, d = c.shape
    n = w.shape[1]
    return pl.pallas_call(
        _mod_kernel,
        grid=(n // tn,),
        in_specs=[pl.BlockSpec((r, d), lambda j: (0, 0)),
                  pl.BlockSpec((d, tn), lambda j: (0, j)),
                  pl.BlockSpec((1, tn), lambda j: (0, j))],
        out_specs=pl.BlockSpec((r, tn), lambda j: (0, j)),
        out_shape=jax.ShapeDtypeStruct((r, n), F32),
        compiler_params=_cparams(("parallel",)),
        name="modulation",
    )(c, w, b)


def _ada_norm(x, g, sc, sh):
    ms = jnp.mean(x * x, axis=-1, keepdims=True)
    y = x * lax.rsqrt(ms + EPS)
    return (y * g) * (1.0 + sc) + sh


def _inproj_kernel(x_ref, g_ref, sc_ref, sh_ref, w_ref, wg_ref, o_ref, og_ref, h_scr):
    @pl.when(pl.program_id(2) == 0)
    def _():
        h = _ada_norm(x_ref[0], g_ref[...], sc_ref[0], sh_ref[0]).astype(BF16)
        h_scr[...] = h
        og_ref[0] = _dot(h, wg_ref[...])

    o_ref[0] = _dot(h_scr[...], w_ref[...]).astype(BF16)


def _inproj(x, g, sc, sh, w, wg, tm=512, tn=1024):
    bsz, l, d = x.shape
    n = w.shape[1]
    ng = wg.shape[1]
    return pl.pallas_call(
        _inproj_kernel,
        grid=(bsz, l // tm, n // tn),
        in_specs=[pl.BlockSpec((1, tm, d), lambda b, i, j: (b, i, 0)),
                  pl.BlockSpec((1, d), lambda b, i, j: (0, 0)),
                  pl.BlockSpec((1, 1, d), lambda b, i, j: (b, 0, 0)),
                  pl.BlockSpec((1, 1, d), lambda b, i, j: (b, 0, 0)),
                  pl.BlockSpec((d, tn), lambda b, i, j: (0, j)),
                  pl.BlockSpec((d, ng), lambda b, i, j: (0, 0))],
        out_specs=[pl.BlockSpec((1, tm, tn), lambda b, i, j: (b, i, j)),
                   pl.BlockSpec((1, tm, ng), lambda b, i, j: (b, i, 0))],
        out_shape=[jax.ShapeDtypeStruct((bsz, l, n), BF16),
                   jax.ShapeDtypeStruct((bsz, l, ng), F32)],
        scratch_shapes=[pltpu.VMEM((tm, d), BF16)],
        compiler_params=_cparams(("parallel", "parallel", "arbitrary")),
        name="inproj",
    )(x, g, sc, sh, w, wg)


def _log_sigmoid(x):
    return jnp.minimum(x, 0.0) - jnp.log(1.0 + jnp.exp(-jnp.abs(x)))


def _mlstm_chunk(q, k, v, i_col, f_col, i_row, f_row, c_st, n_st, m_st, reverse):
    t = q.shape[0]
    kscale = MLSTM_HEAD_DIM ** -0.5
    r_i = lax.broadcasted_iota(jnp.int32, (t, t), 0)
    c_i = lax.broadcasted_iota(jnp.int32, (t, t), 1)
    seen = (c_i >= r_i) if reverse else (c_i <= r_i)
    seen_t = (r_i >= c_i) if reverse else (r_i <= c_i)
    b_col = jnp.sum(jnp.where(seen, f_row, 0.0), axis=1, keepdims=True)
    b_row = jnp.sum(jnp.where(seen_t, f_col, 0.0), axis=0, keepdims=True)
    d = jnp.where(seen, b_col - b_row + i_row, NEG_INF)
    inter = b_col + m_st
    m_t = jnp.maximum(inter, jnp.max(d, axis=1, keepdims=True))
    w_inter = jnp.exp(inter - m_t)
    p = jnp.exp(d - m_t) * (_dot_nt(q, k) * kscale)
    num = w_inter * _dot(q, c_st.astype(BF16)) + _dot(p.astype(BF16), v)
    den = (w_inter * jnp.sum(q.astype(F32) * n_st, axis=1, keepdims=True)
           + jnp.sum(p, axis=1, keepdims=True))
    h = num / jnp.maximum(jnp.abs(den), jnp.exp(-m_t))
    b_last = b_col[0:1] if reverse else b_col[t - 1:t]
    dec = b_last - b_col + i_col
    m_new = jnp.maximum(b_last + m_st, jnp.max(dec, axis=0, keepdims=True))
    w_old = jnp.exp(b_last + m_st - m_new)
    w_s = jnp.exp(dec - m_new)
    kw = k.astype(F32) * (w_s * kscale)
    c_new = w_old * c_st + _dot(kw.T.astype(BF16), v)
    n_new = w_old * n_st + jnp.sum(kw, axis=0, keepdims=True)
    return h, c_new, n_new, m_new


def _mlstm_kernel(*refs, reverse, hb):
    if reverse:
        q_ref, k_ref, v_ref, gc_ref, gr_ref, bc_ref, br_ref, o_ref, c_scr, n_scr, m_scr = refs
    else:
        (q_ref, k_ref, v_ref, gc_ref, gr_ref, bc_ref, br_ref, hb_ref, mo_ref, ng_ref,
         o_ref, c_scr, n_scr, m_scr) = refs

    @pl.when(pl.program_id(2) == 0)
    def _():
        c_scr[...] = jnp.zeros_like(c_scr)
        n_scr[...] = jnp.zeros_like(n_scr)
        m_scr[...] = jnp.zeros_like(m_scr)

    dh = MLSTM_HEAD_DIM
    gi = 2 if reverse else 0
    for hh in range(hb):
        cols = slice(hh * dh, (hh + 1) * dh)
        gc = gc_ref[0, hh] + bc_ref[hh]
        gr = gr_ref[0, hh] + br_ref[hh]
        h, c_new, n_new, m_new = _mlstm_chunk(
            q_ref[0, :, cols], k_ref[0, :, cols], v_ref[0, :, cols],
            gc[:, gi:gi + 1], _log_sigmoid(gc[:, gi + 1:gi + 2]),
            gr[gi:gi + 1, :], _log_sigmoid(gr[gi + 1:gi + 2, :]),
            c_scr[hh], n_scr[hh], m_scr[hh], reverse)
        c_scr[hh] = c_new
        n_scr[hh] = n_new
        m_scr[hh] = m_new
        if reverse:
            o_ref[0, :, cols] = h
        else:
            ht = h + hb_ref[0, :, cols]
            ht = ht * lax.rsqrt(jnp.mean(ht * ht, axis=-1, keepdims=True) + EPS)
            ht = ht * ng_ref[:, cols]
            o_ref[0, :, cols] = (ht * jax.nn.sigmoid(mo_ref[0, :, cols].astype(F32))).astype(BF16)


def _mlstm_direction(pm, gc, gr, bc, br, reverse, h_bwd=None, norm_g=None, t=256, hb=2):
    bsz, l, _ = pm.shape
    nh = MLSTM_HEADS
    dh = MLSTM_HEAD_DIM
    w = nh * dh
    nc = l // t
    ng = nh // hb
    wb = hb * dh

    def ci(c):
        return (nc - 1 - c) if reverse else c

    in_specs = [pl.BlockSpec((1, t, wb), lambda b, g, c: (b, ci(c), g)),
                pl.BlockSpec((1, t, wb), lambda b, g, c: (b, ci(c), ng + g)),
                pl.BlockSpec((1, t, wb), lambda b, g, c: (b, ci(c), 2 * ng + g)),
                pl.BlockSpec((1, hb, t, 4), lambda b, g, c: (b, g, ci(c), 0)),
                pl.BlockSpec((1, hb, 4, t), lambda b, g, c: (b, g, 0, ci(c))),
                pl.BlockSpec((hb, 1, 4), lambda b, g, c: (g, 0, 0)),
                pl.BlockSpec((hb, 4, 1), lambda b, g, c: (g, 0, 0))]
    args = [pm, pm, pm, gc, gr, bc, br]
    if not reverse:
        in_specs += [pl.BlockSpec((1, t, wb), lambda b, g, c: (b, c, g)),
                     pl.BlockSpec((1, t, wb), lambda b, g, c: (b, c, 3 * ng + g)),
                     pl.BlockSpec((1, wb), lambda b, g, c: (0, g))]
        args += [h_bwd, pm, norm_g]
    return pl.pallas_call(
        functools.partial(_mlstm_kernel, reverse=reverse, hb=hb),
        grid=(bsz, ng, nc),
        in_specs=in_specs,
        out_specs=pl.BlockSpec((1, t, wb), lambda b, g, c: (b, ci(c), g)),
        out_shape=jax.ShapeDtypeStruct((bsz, l, w), F32 if reverse else BF16),
        scratch_shapes=[pltpu.VMEM((hb, dh, dh), F32),
                        pltpu.VMEM((hb, 1, dh), F32),
                        pltpu.VMEM((hb, 1, 1), F32)],
        compiler_params=_cparams(("parallel", "parallel", "arbitrary")),
        name="mlstm_bwd" if reverse else "mlstm_fwd",
    )(*args)


def _na_bias_table(rpb):
    c = jnp.arange(GRID_W)
    cs = jnp.clip(c - NA_WIN_COLS // 2, 0, GRID_W - NA_WIN_COLS)
    col_ok = (c[None, :] >= cs[:, None]) & (c[None, :] < cs[:, None] + NA_WIN_COLS)
    dc = jnp.clip(c[None, :] - c[:, None], -(NA_WIN_COLS - 1), NA_WIN_COLS - 1) + (NA_WIN_COLS - 1)
    onehot = (dc[None] == jnp.arange(2 * NA_WIN_COLS - 1)[:, None, None]).astype(F32)
    t1 = jnp.einsum("hdc,cqk->hdqk", rpb.astype(F32), onehot, precision=lax.Precision.HIGHEST)
    t1 = jnp.where(col_ok[None, None], t1, NEG_INF)
    fill = jnp.full_like(t1[:, :1], NEG_INF)
    lo = jnp.concatenate([fill, t1], axis=1)
    hi = jnp.concatenate([t1, fill], axis=1)
    return jnp.concatenate([lo, hi], axis=-1)


NA_RBLK = 8
NA_KROWS = 2 * NA_WIN_ROWS


def _na_block(q_ref, kw, vw, tp_ref, o_ref, p_scr, shift, starts):
    w2 = 2 * GRID_W
    scale = NA_HEAD_DIM ** -0.5
    s = _dot_nt(q_ref[0], kw) * scale
    lane = lax.broadcasted_iota(jnp.int32, (GRID_W, w2), 1)
    inv_l = []
    for jr in range(NA_RBLK):
        rows = slice(jr * GRID_W, (jr + 1) * GRID_W)
        st = starts[jr]
        blocks = {}
        for pr in range(NA_KROWS // 2):
            ok0 = st <= 2 * pr < st + NA_WIN_ROWS
            ok1 = st <= 2 * pr + 1 < st + NA_WIN_ROWS
            if not (ok0 or ok1):
                p_scr[rows, pr * w2:(pr + 1) * w2] = jnp.zeros((GRID_W, w2), BF16)
                continue
            d0 = shift + 2 * pr - jr + (NA_WIN_ROWS - 1)
            x = s[rows, pr * w2:(pr + 1) * w2] + tp_ref[0, d0 + 1]
            if not ok0:
                x = jnp.where(lane >= GRID_W, x, NEG_INF)
            if not ok1:
                x = jnp.where(lane < GRID_W, x, NEG_INF)
            blocks[pr] = x
        m = functools.reduce(jnp.maximum, blocks.values())
        m = jnp.max(m, axis=-1, keepdims=True)
        tot = None
        for pr, x in blocks.items():
            p = jnp.exp(x - m)
            tot = p if tot is None else tot + p
            p_scr[rows, pr * w2:(pr + 1) * w2] = p.astype(BF16)
        inv_l.append(1.0 / jnp.sum(tot, axis=-1, keepdims=True))
    o = _dot(p_scr[...], vw)
    for jr in range(NA_RBLK):
        rows = slice(jr * GRID_W, (jr + 1) * GRID_W)
        o_ref[0, rows, :] = (o[rows] * inv_l[jr]).astype(BF16)


def _na_kernel(q_ref, k_ref, v_ref, tp_ref, o_ref, p_scr, *, rows):
    i = pl.program_id(2)
    nblk = rows // NA_RBLK
    half = NA_WIN_ROWS // 2
    kwin = NA_KROWS * GRID_W

    def run(base, shift, starts):
        kpos = base * GRID_W
        if not isinstance(kpos, int):
            kpos = pl.multiple_of(kpos, GRID_W)
        _na_block(q_ref, k_ref[0, pl.ds(kpos, kwin), :], v_ref[0, pl.ds(kpos, kwin), :],
                  tp_ref, o_ref, p_scr, shift, starts)

    @pl.when(i == 0)
    def _():
        run(0, 0, [max(jr - half, 0) for jr in range(NA_RBLK)])

    @pl.when(i == nblk - 1)
    def _():
        run(rows - NA_KROWS, NA_RBLK - NA_KROWS,
            [min(jr + half, NA_KROWS - NA_WIN_ROWS) for jr in range(NA_RBLK)])

    @pl.when(jnp.logical_and(i > 0, i < nblk - 1))
    def _():
        run(i * NA_RBLK - half, -half, list(range(NA_RBLK)))


def _neighbourhood_attention(pn, bias_tab, col0):
    bsz, l, _ = pn.shape
    nh, dh = NA_HEADS, NA_HEAD_DIM
    rows = l // GRID_W
    assert rows >= NA_KROWS and rows % NA_RBLK == 0
    tq = NA_RBLK * GRID_W
    return pl.pallas_call(
        functools.partial(_na_kernel, rows=rows),
        grid=(bsz, nh, rows // NA_RBLK),
        in_specs=[pl.BlockSpec((1, tq, dh), lambda b, h, i: (b, i, col0 + h)),
                  pl.BlockSpec((1, l, dh), lambda b, h, i: (b, 0, col0 + nh + h)),
                  pl.BlockSpec((1, l, dh), lambda b, h, i: (b, 0, col0 + 2 * nh + h)),
                  pl.BlockSpec((1, 2 * NA_WIN_ROWS, GRID_W, 2 * GRID_W), lambda b, h, i: (h, 0, 0, 0))],
        out_specs=pl.BlockSpec((1, tq, dh), lambda b, h, i: (b, i, h)),
        out_shape=jax.ShapeDtypeStruct((bsz, l, nh * dh), BF16),
        scratch_shapes=[pltpu.VMEM((tq, NA_KROWS * GRID_W), BF16)],
        compiler_params=_cparams(("parallel", "parallel", "arbitrary")),
        name="natten",
    )(pn, pn, pn, bias_tab)


def _outproj_kernel(hm_ref, hn_ref, w1_ref, w2_ref, x_ref, g_ref, o_ref):
    acc = _dot(hm_ref[0], w1_ref[...]) + _dot(hn_ref[0], w2_ref[...])
    o_ref[0] = x_ref[0] + g_ref[0] * acc


def _outproj(hm, hn, w, x, g, tm=512, tn=1024):
    bsz, l, d = x.shape
    kh = hm.shape[2]
    return pl.pallas_call(
        _outproj_kernel,
        grid=(bsz, l // tm, d // tn),
        in_specs=[pl.BlockSpec((1, tm, kh), lambda b, i, j: (b, i, 0)),
                  pl.BlockSpec((1, tm, kh), lambda b, i, j: (b, i, 0)),
                  pl.BlockSpec((kh, tn), lambda b, i, j: (0, j)),
                  pl.BlockSpec((kh, tn), lambda b, i, j: (1, j)),
                  pl.BlockSpec((1, tm, tn), lambda b, i, j: (b, i, j)),
                  pl.BlockSpec((1, 1, tn), lambda b, i, j: (b, 0, j))],
        out_specs=pl.BlockSpec((1, tm, tn), lambda b, i, j: (b, i, j)),
        out_shape=jax.ShapeDtypeStruct((bsz, l, d), F32),
        compiler_params=_cparams(("parallel", "parallel", "parallel")),
        name="outproj",
    )(hm, hn, w, w, x, g)


def _top_values_axis0(chunks, k, group):
    vals = [[] for _ in chunks]
    for g0 in range(0, len(chunks), group):
        ids = range(g0, min(g0 + group, len(chunks)))
        state = {c: chunks[c] for c in ids}
        for _ in range(k):
            for c in ids:
                mx = jnp.max(state[c], axis=0, keepdims=True)
                vals[c].append(mx)
                state[c] = jnp.where(state[c] == mx, NEG_INF, state[c])
    return vals


def _count_reaching(rows, x, strict):
    cnt = jnp.zeros_like(x)
    for row in rows:
        cnt = cnt + jnp.where((row > x) if strict else (row >= x), 1.0, 0.0)
    return cnt


_CAND_PAIRS = [(a, b) for a in range(PEER_TOPK + 1) for b in range(PEER_TOPK + 1)
               if (a + 1) * (b + 1) <= PEER_TOPK + 1]
_CAND_ROWS = -(-len(_CAND_PAIRS) // 8) * 8


def _split2(x):
    hi = x.astype(BF16)
    return hi, (x - hi.astype(F32)).astype(BF16)


def _dot_f32(a, b):
    a0, a1 = _split2(a)
    b0, b1 = _split2(b)
    return _dot(a0, b0) + (_dot(a0, b1) + _dot(a1, b0))


def _route_kernel(x_ref, g_ref, sc_ref, sh_ref, wq_ref, k1_ref, k2_ref,
                  hq_ref, st_ref, cnt_ref, a_ref, r2_ref, b_ref, ht_scr, cand_scr):
    @pl.when(pl.program_id(2) == 0)
    def _():
        h = _ada_norm(x_ref[0], g_ref[...], sc_ref[0], sh_ref[0])
        ht = h.T
        ht_scr[...] = ht.astype(BF16)
        st = jnp.maximum(jnp.max(jnp.abs(ht), axis=0, keepdims=True), FP8_TINY) * (1.0 / FP8_MAX)
        st_ref[0] = st
        hq_ref[0] = (ht / st).astype(FP8)

    qt = _dot(wq_ref[...], ht_scr[...])
    s1 = _dot_f32(k1_ref[0], qt[:PEER_HALF])
    s2 = _dot_f32(k2_ref[0], qt[PEER_HALF:])
    nchunk = s1.shape[1] // LANES
    lanes = [slice(c * LANES, (c + 1) * LANES) for c in range(nchunk)]
    s1c = [s1[:, ln] for ln in lanes]
    s2c = [s2[:, ln] for ln in lanes]
    ktop = PEER_TOPK + 1
    v1 = _top_values_axis0(s1c, ktop, group=2)
    v2 = _top_values_axis0(s2c, ktop, group=2)
    cand_scr[...] = jnp.full(cand_scr.shape, NEG_INF, F32)
    for c in range(nchunk):
        for i, (ka, kb) in enumerate(_CAND_PAIRS):
            cand_scr[i:i + 1, lanes[c]] = v1[c][ka] + v2[c][kb]
    sc = _top_values_axis0([cand_scr[:, ln] for ln in lanes], ktop, group=nchunk)
    for c in range(nchunk):
        z = jnp.ones_like(sc[c][0])
        for kk in range(1, PEER_TOPK):
            z = z + jnp.exp(sc[c][kk] - sc[c][0])
        thr = 0.5 * (sc[c][PEER_TOPK - 1] + sc[c][PEER_TOPK]) - s1c[c]
        cnt_ref[0, 0, :, lanes[c]] = _count_reaching(v2[c], thr, strict=False)
        r2_ref[0, 0, :, lanes[c]] = _count_reaching(v2[c], s2c[c], strict=True).astype(BF16)
        a_ref[0, 0, :, lanes[c]] = jnp.exp(s1c[c] - v1[c][0]) / z
        b_ref[0, 0, :, lanes[c]] = jnp.exp(s2c[c] - v2[c][0]).astype(BF16)


def _peer_route(x, g, sc, sh, wq_t, k1, k2, tm=512):
    bsz, l, d = x.shape
    nh = PEER_HEADS
    qd = 2 * PEER_HALF
    nk = PEER_N_KEYS
    tab = jax.ShapeDtypeStruct((bsz, nh, nk, l), F32)
    tab16 = jax.ShapeDtypeStruct((bsz, nh, nk, l), BF16)
    tab_spec = pl.BlockSpec((1, 1, nk, tm), lambda b, i, h: (b, h, 0, i))
    return pl.pallas_call(
        _route_kernel,
        grid=(bsz, l // tm, nh),
        in_specs=[pl.BlockSpec((1, tm, d), lambda b, i, h: (b, i, 0)),
                  pl.BlockSpec((1, d), lambda b, i, h: (0, 0)),
                  pl.BlockSpec((1, 1, d), lambda b, i, h: (b, 0, 0)),
                  pl.BlockSpec((1, 1, d), lambda b, i, h: (b, 0, 0)),
                  pl.BlockSpec((qd, d), lambda b, i, h: (h, 0)),
                  pl.BlockSpec((1, nk, PEER_HALF), lambda b, i, h: (h, 0, 0)),
                  pl.BlockSpec((1, nk, PEER_HALF), lambda b, i, h: (h, 0, 0))],
        out_specs=[pl.BlockSpec((1, d, tm), lambda b, i, h: (b, 0, i)),
                   pl.BlockSpec((1, 1, tm), lambda b, i, h: (b, 0, i)),
                   tab_spec, tab_spec, tab_spec, tab_spec],
        out_shape=[jax.ShapeDtypeStruct((bsz, d, l), FP8), jax.ShapeDtypeStruct((bsz, 1, l), F32),
                   tab, tab, tab16, tab16],
        scratch_shapes=[pltpu.VMEM((d, tm), BF16), pltpu.VMEM((_CAND_ROWS, tm), F32)],
        compiler_params=_cparams(("parallel", "parallel", "arbitrary")),
        name="peer_route",
    )(x, g, sc, sh, wq_t, k1, k2)


def _gelu_tanh(x):
    return 0.5 * x * (1.0 + jnp.tanh(0.7978845608028654 * (x + 0.044715 * (x * x * x))))


EXPERT_ROUNDS = 8


def _expert_kernel(hq_ref, st_ref, uq_ref, su_ref, vq_ref, sv_ref, cnt_ref, a_ref, r2_ref, b_ref,
                   o_ref, act_scr, w_scr, p16_scr, p8_scr, sp_scr, *, te, tm):
    j = pl.program_id(2)
    nt = pl.num_programs(2) - 2
    nk = PEER_N_KEYS
    cur = j % 2
    prv = 1 - cur
    jt = jnp.minimum(j, nt - 1)
    d = vq_ref.shape[1]
    nlc = tm // LANES

    @pl.when(j == 0)
    def _():
        o_ref[...] = jnp.zeros_like(o_ref)
        act_scr[...] = jnp.zeros_like(act_scr)
        w_scr[...] = jnp.zeros_like(w_scr)
        p8_scr[...] = jnp.zeros_like(p8_scr)
        sp_scr[...] = jnp.zeros_like(sp_scr)

    nr = EXPERT_ROUNDS
    rd, re = d // nr, te // nr
    mh, nh = te // 2, tm // 2
    assert nk % re == 0 and nr == 8
    pmax = [jnp.zeros((BF16_ROWS, LANES), BF16) for _ in range(nlc)]
    for r in range(nr):
        rows = slice(r * rd, (r + 1) * rd)
        o_ref[0, rows, :] += _dot(vq_ref[0, rows, :], p8_scr[cur]) * sp_scr[cur]
        a_idx = jt * (te // nk) + (r * re) // nk
        k0 = (r * re) % nk
        for tc in range(tm // LANES):
            lanes = slice(tc * LANES, (tc + 1) * LANES)
            cnt8 = cnt_ref[0, a_idx, :, lanes]
            fa8 = a_ref[0, a_idx, :, lanes]
            w = [jnp.zeros((BF16_ROWS, LANES), BF16) for _ in range(re // BF16_ROWS)]
            for hh in range(PEER_HEADS):
                cnt = jnp.broadcast_to(cnt8[hh:hh + 1], (BF16_ROWS, LANES)).astype(BF16)
                fa = jnp.broadcast_to(fa8[hh:hh + 1], (BF16_ROWS, LANES)).astype(BF16)
                for sub in range(re // BF16_ROWS):
                    ks = slice(k0 + sub * BF16_ROWS, k0 + (sub + 1) * BF16_ROWS)
                    sel = r2_ref[0, hh, ks, lanes] < cnt
                    w[sub] = w[sub] + jnp.where(sel, b_ref[0, hh, ks, lanes], 0) * fa
            for sub in range(re // BF16_ROWS):
                e0 = r * re + sub * BF16_ROWS
                w_scr[cur, e0:e0 + BF16_ROWS, lanes] = w[sub]
        er = slice(r * re, (r + 1) * re)
        for lc in range(nlc):
            lanes = slice(lc * LANES, (lc + 1) * LANES)
            g = _gelu_tanh(act_scr[prv, er, lanes].astype(BF16))
            pp = w_scr[prv, er, lanes] * (g * sv_ref[er, :])
            p16_scr[er, lanes] = pp
            pa = jnp.abs(pp)
            for sub in range(re // BF16_ROWS):
                pmax[lc] = jnp.maximum(pmax[lc], pa[sub * BF16_ROWS:(sub + 1) * BF16_ROWS])
        if r % 2 == 0:
            qm, qn = (r // 2) // 2, (r // 2) % 2
            er = slice(qm * mh, (qm + 1) * mh)
            acc = _dot(uq_ref[er, :], hq_ref[0, :, qn * nh:(qn + 1) * nh])
            for lc in range(nh // LANES):
                lanes = slice(qn * nh + lc * LANES, qn * nh + (lc + 1) * LANES)
                act_scr[cur, er, lanes] = (acc[:, lc * LANES:(lc + 1) * LANES] * su_ref[er, :]
                                           * st_ref[0, :, lanes])

    for lc in range(nlc):
        lanes = slice(lc * LANES, (lc + 1) * LANES)
        amax = jnp.max(pmax[lc].astype(F32), axis=0, keepdims=True)
        inv = (FP8_MAX / jnp.maximum(amax, FP8_TINY)).astype(BF16)
        sp_scr[prv, :, lanes] = 1.0 / inv.astype(F32)
        p8_scr[prv, :, lanes] = (p16_scr[:, lanes] * inv).astype(FP8)


PEER_TE = 1024


def _peer_experts(hq, st, uq, su, vq, sv, cnt, fa, r2, fb, tm=512):
    bsz, d, l = hq.shape
    ne = uq.shape[0]
    nh, nk = PEER_HEADS, PEER_N_KEYS
    te = PEER_TE
    nt = ne // te
    tab_spec = pl.BlockSpec((1, nh, nk, tm), lambda b, i, j: (b, 0, 0, i),
                            pipeline_mode=pl.Buffered(1))
    tab_a_spec = pl.BlockSpec((1, nk, nh, tm), lambda b, i, j: (b, 0, 0, i),
                              pipeline_mode=pl.Buffered(1))
    return pl.pallas_call(
        functools.partial(_expert_kernel, te=te, tm=tm),
        grid=(bsz, l // tm, nt + 2),
        in_specs=[pl.BlockSpec((1, d, tm), lambda b, i, j: (b, 0, i), pipeline_mode=pl.Buffered(1)),
                  pl.BlockSpec((1, 1, tm), lambda b, i, j: (b, 0, i)),
                  pl.BlockSpec((te, d), lambda b, i, j: (jnp.minimum(j, nt - 1), 0)),
                  pl.BlockSpec((te, LANES), lambda b, i, j: (jnp.minimum(j, nt - 1), 0)),
                  pl.BlockSpec((1, d, te), lambda b, i, j: (jnp.maximum(j - 2, 0), 0, 0)),
                  pl.BlockSpec((te, LANES), lambda b, i, j: (jnp.clip(j - 1, 0, nt - 1), 0)),
                  tab_a_spec, tab_a_spec, tab_spec, tab_spec],
        out_specs=pl.BlockSpec((1, d, tm), lambda b, i, j: (b, 0, i)),
        out_shape=jax.ShapeDtypeStruct((bsz, d, l), F32),
        scratch_shapes=[pltpu.VMEM((2, te, tm), F32), pltpu.VMEM((2, te, tm), BF16),
                        pltpu.VMEM((te, tm), BF16), pltpu.VMEM((2, te, tm), FP8),
                        pltpu.VMEM((2, 1, tm), F32)],
        compiler_params=_cparams(("parallel", "parallel", "arbitrary")),
        name="peer_experts",
    )(hq, st, uq, su, vq, sv, cnt, fa, r2, fb)


def _final_kernel(x_ref, pt_ref, g2_ref, fg_ref, o_ref):
    x = x_ref[0] + g2_ref[0] * pt_ref[0].T
    ms = jnp.mean(x * x, axis=-1, keepdims=True)
    o_ref[0] = (x * lax.rsqrt(ms + EPS)) * fg_ref[...]


def _final(x, peer_t, g2, fg, tm=256):
    bsz, l, d = x.shape
    return pl.pallas_call(
        _final_kernel,
        grid=(bsz, l // tm),
        in_specs=[pl.BlockSpec((1, tm, d), lambda b, i: (b, i, 0)),
                  pl.BlockSpec((1, d, tm), lambda b, i: (b, 0, i)),
                  pl.BlockSpec((1, 1, d), lambda b, i: (b, 0, 0)),
                  pl.BlockSpec((1, d), lambda b, i: (0, 0))],
        out_specs=pl.BlockSpec((1, tm, d), lambda b, i: (b, i, 0)),
        out_shape=jax.ShapeDtypeStruct((bsz, l, d), F32),
        compiler_params=_cparams(("parallel", "parallel")),
        name="final_norm",
    )(x, peer_t, g2, fg)


def _prepare_weights(w_in, gate_b, na_rpb, w_out, peer_wq, peer_k1, peer_k2, peer_u, peer_v):
    mw = MLSTM_HEADS * MLSTM_HEAD_DIM
    n_gates = 4 * MLSTM_HEADS
    w_main = jnp.concatenate([w_in[:, :4 * mw], w_in[:, 4 * mw + n_gates:]], axis=1).astype(BF16)
    w_gate = jnp.pad(w_in[:, 4 * mw:4 * mw + n_gates], ((0, 0), (0, LANES - n_gates))).astype(BF16)
    gb = gate_b.reshape(4, MLSTM_HEADS).T
    su = jnp.maximum(jnp.max(jnp.abs(peer_u), axis=1, keepdims=True), FP8_TINY) * (1.0 / FP8_MAX)
    sv = jnp.maximum(jnp.max(jnp.abs(peer_v), axis=1, keepdims=True), FP8_TINY) * (1.0 / FP8_MAX)
    sv = sv.astype(BF16)
    return dict(
        w_main=w_main, w_gate=w_gate,
        gate_bc=gb[:, None, :], gate_br=gb[:, :, None],
        na_bias=_na_bias_table(na_rpb),
        w_out=w_out.astype(BF16),
        wq_t=peer_wq.T.astype(BF16),
        k1=peer_k1, k2=peer_k2,
        uq=(peer_u / su).astype(FP8), su=jnp.broadcast_to(su, (su.shape[0], LANES)),
        vq=(peer_v / sv.astype(F32)).astype(FP8).reshape(-1, PEER_TE, peer_v.shape[1]).transpose(0, 2, 1),
        sv=jnp.broadcast_to(sv, (sv.shape[0], LANES)))


def _trunk(x, mod, norm1_g, mlstm_norm_g, norm2_g, final_g, wts):
    bsz, l, d = x.shape
    sh1, sc1, g1, sh2, sc2, g2 = [m[:, None, :] for m in jnp.split(mod, 6, axis=-1)]
    mw = MLSTM_HEADS * MLSTM_HEAD_DIM
    proj, gates = _inproj(x, norm1_g, sc1, sh1, wts["w_main"], wts["w_gate"])
    g4 = gates[:, :, :4 * MLSTM_HEADS].reshape(bsz, l, 4, MLSTM_HEADS)
    gc = g4.transpose(0, 3, 1, 2)
    gr = g4.transpose(0, 3, 2, 1)
    t = min(256, l)
    h_bwd = _mlstm_direction(proj, gc, gr, wts["gate_bc"], wts["gate_br"], True, t=t)
    hm = _mlstm_direction(proj, gc, gr, wts["gate_bc"], wts["gate_br"], False,
                          h_bwd=h_bwd, norm_g=mlstm_norm_g, t=t)
    hn = _neighbourhood_attention(proj, wts["na_bias"], 4 * mw // NA_HEAD_DIM)
    x1 = _outproj(hm, hn, wts["w_out"], x, g1)
    hq, st, cnt, fa, r2, fb = _peer_route(x1, norm2_g, sc2, sh2, wts["wq_t"], wts["k1"], wts["k2"])
    peer_t = _peer_experts(hq, st, wts["uq"], wts["su"], wts["vq"], wts["sv"],
                           cnt.transpose(0, 2, 1, 3), fa.transpose(0, 2, 1, 3), r2, fb)
    return _final(x1, peer_t, g2, final_g)


def kernel(x_prompt, x_sample, c_prompt, c_sample, ada_w, ada_b, norm1_g, w_in, gate_b, mlstm_norm_g, na_rpb, w_out, norm2_g, peer_wq, peer_k1, peer_k2, peer_u, peer_v, final_g):
    assert ada_w.shape[0] == 1, "single-layer trunk"
    nb_p = c_prompt.shape[0]
    nb_s = c_sample.shape[0]
    rows = -(-(nb_p + nb_s) // 8) * 8
    c_all = jnp.concatenate([c_prompt, c_sample], axis=0)
    c_all = jnp.pad(c_all, ((0, rows - nb_p - nb_s), (0, 0)))
    mod = _modulation(c_all, ada_w[0], ada_b)
    wts = _prepare_weights(w_in[0], gate_b[0], na_rpb[0], w_out[0], peer_wq[0], peer_k1[0],
                           peer_k2[0], peer_u[0], peer_v[0])
    fg = final_g[None, :]
    y_prompt = _trunk(x_prompt, mod[:nb_p], norm1_g, mlstm_norm_g, norm2_g, fg, wts)
    y_sample = _trunk(x_sample, mod[nb_p:nb_p + nb_s], norm1_g, mlstm_norm_g, norm2_g, fg, wts)
    return (y_prompt, y_sample)
```

```python
import functools

import jax
import jax.numpy as jnp
from jax import lax
from jax.experimental import pallas as pl
from jax.experimental.pallas import tpu as pltpu

F32 = jnp.float32
BF16 = jnp.bfloat16
FP8 = jnp.float8_e4m3fn
FP8_MAX = 448.0
FP8_TINY = 1e-30

GRID_W = 64
MLSTM_HEADS = 8
MLSTM_HEAD_DIM = 256
NA_HEADS = 16
NA_HEAD_DIM = 128
NA_WIN_ROWS = 8
NA_WIN_COLS = 16
PEER_HEADS = 8
PEER_HALF = 128
PEER_N_KEYS = 128
PEER_TOPK = 16
EPS = 1e-6

LANES = 128
BF16_ROWS = 16
VMEM_LIMIT = 56 * 1024 * 1024

NEG_INF = float("-inf")


def _cparams(sem):
    return pltpu.CompilerParams(dimension_semantics=sem, vmem_limit_bytes=VMEM_LIMIT)


def _dot(a, b):
    return jnp.dot(a, b, preferred_element_type=F32)


def _dot_nt(a, b):
    return lax.dot_general(a, b, (((1,), (1,)), ((), ())), preferred_element_type=F32)


def _mod_kernel(c_ref, w_ref, b_ref, o_ref):
    c = c_ref[...]
    s = c * jax.nn.sigmoid(c)
    hi = s.astype(BF16)
    lo = (s - hi.astype(F32)).astype(BF16)
    w = w_ref[...].astype(BF16)
    o_ref[...] = _dot(hi, w) + _dot(lo, w) + b_ref[...]


def _modulation(c, w, b, tn=512):
    r, d = c.shape
    n = w.shape[1]
    return pl.pallas_call(
        _mod_kernel,
        grid=(n // tn,),
        in_specs=[pl.BlockSpec((r, d), lambda j: (0, 0)),
                  pl.BlockSpec((d, tn), lambda j: (0, j)),
                  pl.BlockSpec((1, tn), lambda j: (0, j))],
        out_specs=pl.BlockSpec((r, tn), lambda j: (0, j)),
        out_shape=jax.ShapeDtypeStruct((r, n), F32),
        compiler_params=_cparams(("parallel",)),
        name="modulation",
    )(c, w, b)


def _ada_norm(x, g, sc, sh):
    ms = jnp.mean(x * x, axis=-1, keepdims=True)
    y = x * lax.rsqrt(ms + EPS)
    return (y * g) * (1.0 + sc) + sh


def _inproj_kernel(x_ref, g_ref, sc_ref, sh_ref, w_ref, wg_ref, o_ref, og_ref, h_scr):
    @pl.when(pl.program_id(2) == 0)
    def _():
        h = _ada_norm(x_ref[0], g_ref[...], sc_ref[0], sh_ref[0]).astype(BF16)
        h_scr[...] = h
        og_ref[0] = _dot(h, wg_ref[...])

    o_ref[0] = _dot(h_scr[...], w_ref[...]).astype(BF16)


def _inproj(x, g, sc, sh, w, wg, tm=512, tn=1024):
    bsz, l, d = x.shape
    n = w.shape[1]
    ng = wg.shape[1]
    return pl.pallas_call(
        _inproj_kernel,
        grid=(bsz, l // tm, n // tn),
        in_specs=[pl.BlockSpec((1, tm, d), lambda b, i, j: (b, i, 0)),
                  pl.BlockSpec((1, d), lambda b, i, j: (0, 0)),
                  pl.BlockSpec((1, 1, d), lambda b, i, j: (b, 0, 0)),
                  pl.BlockSpec((1, 1, d), lambda b, i, j: (b, 0, 0)),
                  pl.BlockSpec((d, tn), lambda b, i, j: (0, j)),
                  pl.BlockSpec((d, ng), lambda b, i, j: (0, 0))],
        out_specs=[pl.BlockSpec((1, tm, tn), lambda b, i, j: (b, i, j)),
                   pl.BlockSpec((1, tm, ng), lambda b, i, j: (b, i, 0))],
        out_shape=[jax.ShapeDtypeStruct((bsz, l, n), BF16),
                   jax.ShapeDtypeStruct((bsz, l, ng), F32)],
        scratch_shapes=[pltpu.VMEM((tm, d), BF16)],
        compiler_params=_cparams(("parallel", "parallel", "arbitrary")),
        name="inproj",
    )(x, g, sc, sh, w, wg)


def _log_sigmoid(x):
    return jnp.minimum(x, 0.0) - jnp.log(1.0 + jnp.exp(-jnp.abs(x)))


def _split3(x):
    hi = x.astype(BF16)
    r = x - hi.astype(F32)
    mid = r.astype(BF16)
    lo = (r - mid.astype(F32)).astype(BF16)
    return hi, mid, lo


def _gate_prep_kernel(g_ref, b_ref, o_ref):
    g = g_ref[0] + b_ref[...]
    t = g.shape[0]
    nh = MLSTM_HEADS
    r_i = lax.broadcasted_iota(jnp.int32, (t, t), 0)
    c_i = lax.broadcasted_iota(jnp.int32, (t, t), 1)
    lower = jnp.where(c_i <= r_i, 1.0, 0.0).astype(BF16)
    upper = jnp.where(c_i >= r_i, 1.0, 0.0).astype(BF16)
    pieces = _split3(_log_sigmoid(g))
    pre = sum(_dot(lower, p) for p in pieces)
    suf = sum(_dot(upper, p) for p in pieces)
    lane = lax.broadcasted_iota(jnp.int32, g.shape, 1)
    out = jnp.where((lane >= nh) & (lane < 2 * nh), pre, g)
    o_ref[0] = jnp.where((lane >= 3 * nh) & (lane < 4 * nh), suf, out)


def _gate_prep(gates, bias, t):
    bsz, l, n = gates.shape
    return pl.pallas_call(
        _gate_prep_kernel,
        grid=(bsz, l // t),
        in_specs=[pl.BlockSpec((1, t, n), lambda b, c: (b, c, 0)),
                  pl.BlockSpec((1, n), lambda b, c: (0, 0))],
        out_specs=pl.BlockSpec((1, t, n), lambda b, c: (b, c, 0)),
        out_shape=jax.ShapeDtypeStruct((bsz, l, n), F32),
        compiler_params=_cparams(("parallel", "parallel")),
        name="gate_prep",
    )(gates, bias)


def _mlstm_chunk(q, k, v, i_col, b_col, i_row, b_row, c_st, n_st, m_st, reverse):
    t = q.shape[0]
    kscale = MLSTM_HEAD_DIM ** -0.5
    r_i = lax.broadcasted_iota(jnp.int32, (t, t), 0)
    c_i = lax.broadcasted_iota(jnp.int32, (t, t), 1)
    seen = (c_i >= r_i) if reverse else (c_i <= r_i)
    d = jnp.where(seen, b_col - b_row + i_row, NEG_INF)
    inter = b_col + m_st
    m_t = jnp.maximum(inter, jnp.max(d, axis=1, keepdims=True))
    w_inter = jnp.exp(inter - m_t)
    p = jnp.exp(d - m_t) * (_dot_nt(q, k) * kscale)
    num = w_inter * _dot(q, c_st.astype(BF16)) + _dot(p.astype(BF16), v)
    den = (w_inter * jnp.sum(q.astype(F32) * n_st, axis=1, keepdims=True)
           + jnp.sum(p, axis=1, keepdims=True))
    h = num / jnp.maximum(jnp.abs(den), jnp.exp(-m_t))
    b_last = b_col[0:1] if reverse else b_col[t - 1:t]
    dec = b_last - b_col + i_col
    m_new = jnp.maximum(b_last + m_st, jnp.max(dec, axis=0, keepdims=True))
    w_old = jnp.exp(b_last + m_st - m_new)
    w_s = jnp.exp(dec - m_new)
    kw = k.astype(F32) * (w_s * kscale)
    c_new = w_old * c_st + _dot(kw.T.astype(BF16), v)
    n_new = w_old * n_st + jnp.sum(kw, axis=0, keepdims=True)
    return h, c_new, n_new, m_new


def _mlstm_kernel(*refs, reverse, hb):
    if reverse:
        q_ref, k_ref, v_ref, gc_ref, gr_ref, o_ref, c_scr, n_scr, m_scr = refs
    else:
        (q_ref, k_ref, v_ref, gc_ref, gr_ref, hb_ref, mo_ref, ng_ref,
         o_ref, c_scr, n_scr, m_scr) = refs

    @pl.when(pl.program_id(2) == 0)
    def _():
        c_scr[...] = jnp.zeros_like(c_scr)
        n_scr[...] = jnp.zeros_like(n_scr)
        m_scr[...] = jnp.zeros_like(m_scr)

    dh = MLSTM_HEAD_DIM
    gi = 2 if reverse else 0
    outs, cs, ns, ms = [], [], [], []
    for hh in range(hb):
        cols = slice(hh * dh, (hh + 1) * dh)
        gc = gc_ref[0, hh]
        gr = gr_ref[0, hh]
        h, c_new, n_new, m_new = _mlstm_chunk(
            q_ref[0, :, cols], k_ref[0, :, cols], v_ref[0, :, cols],
            gc[:, gi:gi + 1], gc[:, gi + 1:gi + 2], gr[gi:gi + 1, :], gr[gi + 1:gi + 2, :],
            c_scr[hh], n_scr[hh], m_scr[hh], reverse)
        cs.append(c_new)
        ns.append(n_new)
        ms.append(m_new)
        if reverse:
            outs.append(h)
        else:
            ht = h + hb_ref[0, :, cols]
            ht = ht * lax.rsqrt(jnp.mean(ht * ht, axis=-1, keepdims=True) + EPS)
            ht = ht * ng_ref[:, cols]
            outs.append((ht * jax.nn.sigmoid(mo_ref[0, :, cols].astype(F32))).astype(BF16))
    o_ref[0] = jnp.concatenate(outs, axis=1)
    c_scr[...] = jnp.stack(cs)
    n_scr[...] = jnp.stack(ns)
    m_scr[...] = jnp.stack(ms)


def _mlstm_direction(pm, gc, gr, reverse, h_bwd=None, norm_g=None, t=256, hb=2):
    bsz, l, _ = pm.shape
    nh = MLSTM_HEADS
    dh = MLSTM_HEAD_DIM
    w = nh * dh
    nc = l // t
    ng = nh // hb
    wb = hb * dh

    def ci(c):
        return (nc - 1 - c) if reverse else c

    in_specs = [pl.BlockSpec((1, t, wb), lambda b, g, c: (b, ci(c), g)),
                pl.BlockSpec((1, t, wb), lambda b, g, c: (b, ci(c), ng + g)),
                pl.BlockSpec((1, t, wb), lambda b, g, c: (b, ci(c), 2 * ng + g)),
                pl.BlockSpec((1, hb, t, 4), lambda b, g, c: (b, g, ci(c), 0)),
                pl.BlockSpec((1, hb, 4, t), lambda b, g, c: (b, g, 0, ci(c)))]
    args = [pm, pm, pm, gc, gr]
    if not reverse:
        in_specs += [pl.BlockSpec((1, t, wb), lambda b, g, c: (b, c, g)),
                     pl.BlockSpec((1, t, wb), lambda b, g, c: (b, c, 3 * ng + g)),
                     pl.BlockSpec((1, wb), lambda b, g, c: (0, g))]
        args += [h_bwd, pm, norm_g]
    return pl.pallas_call(
        functools.partial(_mlstm_kernel, reverse=reverse, hb=hb),
        grid=(bsz, ng, nc),
        in_specs=in_specs,
        out_specs=pl.BlockSpec((1, t, wb), lambda b, g, c: (b, ci(c), g)),
        out_shape=jax.ShapeDtypeStruct((bsz, l, w), F32 if reverse else BF16),
        scratch_shapes=[pltpu.VMEM((hb, dh, dh), F32),
                        pltpu.VMEM((hb, 1, dh), F32),
                        pltpu.VMEM((hb, 1, 1), F32)],
        compiler_params=_cparams(("parallel", "parallel", "arbitrary")),
        name="mlstm_bwd" if reverse else "mlstm_fwd",
    )(*args)


def _na_bias_table(rpb):
    c = jnp.arange(GRID_W)
    cs = jnp.clip(c - NA_WIN_COLS // 2, 0, GRID_W - NA_WIN_COLS)
    col_ok = (c[None, :] >= cs[:, None]) & (c[None, :] < cs[:, None] + NA_WIN_COLS)
    dc = jnp.clip(c[None, :] - c[:, None], -(NA_WIN_COLS - 1), NA_WIN_COLS - 1) + (NA_WIN_COLS - 1)
    onehot = (dc[None] == jnp.arange(2 * NA_WIN_COLS - 1)[:, None, None]).astype(F32)
    t1 = jnp.einsum("hdc,cqk->hdqk", rpb.astype(F32), onehot, precision=lax.Precision.HIGHEST)
    t1 = jnp.where(col_ok[None, None], t1, NEG_INF)
    fill = jnp.full_like(t1[:, :1], NEG_INF)
    lo = jnp.concatenate([fill, t1], axis=1)
    hi = jnp.concatenate([t1, fill], axis=1)
    return jnp.concatenate([lo, hi], axis=-1)


NA_RBLK = 8
NA_KROWS = 2 * NA_WIN_ROWS


def _na_block(q_ref, kw, vw, tp_ref, o_ref, p_scr, shift, starts):
    w2 = 2 * GRID_W
    scale = NA_HEAD_DIM ** -0.5
    s = _dot_nt(q_ref[0], kw) * scale
    lane = lax.broadcasted_iota(jnp.int32, (GRID_W, w2), 1)
    inv_l = []
    for jr in range(NA_RBLK):
        rows = slice(jr * GRID_W, (jr + 1) * GRID_W)
        st = starts[jr]
        blocks = {}
        for pr in range(NA_KROWS // 2):
            ok0 = st <= 2 * pr < st + NA_WIN_ROWS
            ok1 = st <= 2 * pr + 1 < st + NA_WIN_ROWS
            if not (ok0 or ok1):
                p_scr[rows, pr * w2:(pr + 1) * w2] = jnp.zeros((GRID_W, w2), BF16)
                continue
            d0 = shift + 2 * pr - jr + (NA_WIN_ROWS - 1)
            x = s[rows, pr * w2:(pr + 1) * w2] + tp_ref[0, d0 + 1]
            if not ok0:
                x = jnp.where(lane >= GRID_W, x, NEG_INF)
            if not ok1:
                x = jnp.where(lane < GRID_W, x, NEG_INF)
            blocks[pr] = x
        m = functools.reduce(jnp.maximum, blocks.values())
        m = jnp.max(m, axis=-1, keepdims=True)
        tot = None
        for pr, x in blocks.items():
            p = jnp.exp(x - m)
            tot = p if tot is None else tot + p
            p_scr[rows, pr * w2:(pr + 1) * w2] = p.astype(BF16)
        inv_l.append(1.0 / jnp.sum(tot, axis=-1, keepdims=True))
    o = _dot(p_scr[...], vw)
    for jr in range(NA_RBLK):
        rows = slice(jr * GRID_W, (jr + 1) * GRID_W)
        o_ref[0, rows, :] = (o[rows] * inv_l[jr]).astype(BF16)


def _na_kernel(q_ref, k_ref, v_ref, tp_ref, o_ref, p_scr, *, rows):
    i = pl.program_id(2)
    nblk = rows // NA_RBLK
    half = NA_WIN_ROWS // 2
    kwin = NA_KROWS * GRID_W

    def run(base, shift, starts):
        kpos = base * GRID_W
        if not isinstance(kpos, int):
            kpos = pl.multiple_of(kpos, GRID_W)
        _na_block(q_ref, k_ref[0, pl.ds(kpos, kwin), :], v_ref[0, pl.ds(kpos, kwin), :],
                  tp_ref, o_ref, p_scr, shift, starts)

    @pl.when(i == 0)
    def _():
        run(0, 0, [max(jr - half, 0) for jr in range(NA_RBLK)])

    @pl.when(i == nblk - 1)
    def _():
        run(rows - NA_KROWS, NA_RBLK - NA_KROWS,
            [min(jr + half, NA_KROWS - NA_WIN_ROWS) for jr in range(NA_RBLK)])

    @pl.when(jnp.logical_and(i > 0, i < nblk - 1))
    def _():
        run(i * NA_RBLK - half, -half, list(range(NA_RBLK)))


def _neighbourhood_attention(pn, bias_tab, col0):
    bsz, l, _ = pn.shape
    nh, dh = NA_HEADS, NA_HEAD_DIM
    rows = l // GRID_W
    assert rows >= NA_KROWS and rows % NA_RBLK == 0
    tq = NA_RBLK * GRID_W
    return pl.pallas_call(
        functools.partial(_na_kernel, rows=rows),
        grid=(bsz, nh, rows // NA_RBLK),
        in_specs=[pl.BlockSpec((1, tq, dh), lambda b, h, i: (b, i, col0 + h)),
                  pl.BlockSpec((1, l, dh), lambda b, h, i: (b, 0, col0 + nh + h)),
                  pl.BlockSpec((1, l, dh), lambda b, h, i: (b, 0, col0 + 2 * nh + h)),
                  pl.BlockSpec((1, 2 * NA_WIN_ROWS, GRID_W, 2 * GRID_W), lambda b, h, i: (h, 0, 0, 0))],
        out_specs=pl.BlockSpec((1, tq, dh), lambda b, h, i: (b, i, h)),
        out_shape=jax.ShapeDtypeStruct((bsz, l, nh * dh), BF16),
        scratch_shapes=[pltpu.VMEM((tq, NA_KROWS * GRID_W), BF16)],
        compiler_params=_cparams(("parallel", "parallel", "arbitrary")),
        name="natten",
    )(pn, pn, pn, bias_tab)


def _outproj_kernel(hm_ref, hn_ref, w1_ref, w2_ref, x_ref, g_ref, o_ref):
    acc = _dot(hm_ref[0], w1_ref[...]) + _dot(hn_ref[0], w2_ref[...])
    o_ref[0] = x_ref[0] + g_ref[0] * acc


def _outproj(hm, hn, w, x, g, tm=512, tn=1024):
    bsz, l, d = x.shape
    kh = hm.shape[2]
    return pl.pallas_call(
        _outproj_kernel,
        grid=(bsz, l // tm, d // tn),
        in_specs=[pl.BlockSpec((1, tm, kh), lambda b, i, j: (b, i, 0)),
                  pl.BlockSpec((1, tm, kh), lambda b, i, j: (b, i, 0)),
                  pl.BlockSpec((kh, tn), lambda b, i, j: (0, j)),
                  pl.BlockSpec((kh, tn), lambda b, i, j: (1, j)),
                  pl.BlockSpec((1, tm, tn), lambda b, i, j: (b, i, j)),
                  pl.BlockSpec((1, 1, tn), lambda b, i, j: (b, 0, j))],
        out_specs=pl.BlockSpec((1, tm, tn), lambda b, i, j: (b, i, j)),
        out_shape=jax.ShapeDtypeStruct((bsz, l, d), F32),
        compiler_params=_cparams(("parallel", "parallel", "parallel")),
        name="outproj",
    )(hm, hn, w, w, x, g)


def _top_values_axis0(chunks, k, group):
    vals = [[] for _ in chunks]
    for g0 in range(0, len(chunks), group):
        ids = range(g0, min(g0 + group, len(chunks)))
        state = {c: chunks[c] for c in ids}
        for _ in range(k):
            for c in ids:
                mx = jnp.max(state[c], axis=0, keepdims=True)
                vals[c].append(mx)
                state[c] = jnp.where(state[c] == mx, NEG_INF, state[c])
    return vals


def _count_reaching(rows, x, strict):
    cnt = jnp.zeros_like(x)
    for row in rows:
        cnt = cnt + jnp.where((row > x) if strict else (row >= x), 1.0, 0.0)
    return cnt


_CAND_PAIRS = [(a, b) for a in range(PEER_TOPK + 1) for b in range(PEER_TOPK + 1)
               if (a + 1) * (b + 1) <= PEER_TOPK + 1]
_CAND_ROWS = -(-len(_CAND_PAIRS) // 8) * 8


def _split2(x):
    hi = x.astype(BF16)
    return hi, (x - hi.astype(F32)).astype(BF16)


def _dot_f32(a, b):
    a0, a1 = _split2(a)
    b0, b1 = _split2(b)
    return _dot(a0, b0) + (_dot(a0, b1) + _dot(a1, b0))


def _route_kernel(x_ref, g_ref, sc_ref, sh_ref, wq_ref, k1_ref, k2_ref,
                  hq_ref, st_ref, cnt_ref, a_ref, r2_ref, b_ref, ht_scr, cand_scr):
    @pl.when(pl.program_id(2) == 0)
    def _():
        h = _ada_norm(x_ref[0], g_ref[...], sc_ref[0], sh_ref[0])
        ht = h.T
        ht_scr[...] = ht.astype(BF16)
        st = jnp.maximum(jnp.max(jnp.abs(ht), axis=0, keepdims=True), FP8_TINY) * (1.0 / FP8_MAX)
        st_ref[0] = st
        hq_ref[0] = (ht / st).astype(FP8)

    qt = _dot(wq_ref[...], ht_scr[...])
    s1 = _dot_f32(k1_ref[0], qt[:PEER_HALF])
    s2 = _dot_f32(k2_ref[0], qt[PEER_HALF:])
    nchunk = s1.shape[1] // LANES
    lanes = [slice(c * LANES, (c + 1) * LANES) for c in range(nchunk)]
    s1c = [s1[:, ln] for ln in lanes]
    s2c = [s2[:, ln] for ln in lanes]
    ktop = PEER_TOPK + 1
    v1 = _top_values_axis0(s1c, ktop, group=2)
    v2 = _top_values_axis0(s2c, ktop, group=2)
    cand_scr[...] = jnp.full(cand_scr.shape, NEG_INF, F32)
    for c in range(nchunk):
        for i, (ka, kb) in enumerate(_CAND_PAIRS):
            cand_scr[i:i + 1, lanes[c]] = v1[c][ka] + v2[c][kb]
    sc = _top_values_axis0([cand_scr[:, ln] for ln in lanes], ktop, group=nchunk)
    for c in range(nchunk):
        z = jnp.ones_like(sc[c][0])
        for kk in range(1, PEER_TOPK):
            z = z + jnp.exp(sc[c][kk] - sc[c][0])
        thr = 0.5 * (sc[c][PEER_TOPK - 1] + sc[c][PEER_TOPK]) - s1c[c]
        cnt_ref[0, 0, :, lanes[c]] = _count_reaching(v2[c], thr, strict=False)
        r2_ref[0, 0, :, lanes[c]] = _count_reaching(v2[c], s2c[c], strict=True).astype(BF16)
        a_ref[0, 0, :, lanes[c]] = jnp.exp(s1c[c] - v1[c][0]) / z
        b_ref[0, 0, :, lanes[c]] = jnp.exp(s2c[c] - v2[c][0]).astype(BF16)


def _peer_route(x, g, sc, sh, wq_t, k1, k2, tm=512):
    bsz, l, d = x.shape
    nh = PEER_HEADS
    qd = 2 * PEER_HALF
    nk = PEER_N_KEYS
    tab = jax.ShapeDtypeStruct((bsz, nh, nk, l), F32)
    tab16 = jax.ShapeDtypeStruct((bsz, nh, nk, l), BF16)
    tab_spec = pl.BlockSpec((1, 1, nk, tm), lambda b, i, h: (b, h, 0, i))
    return pl.pallas_call(
        _route_kernel,
        grid=(bsz, l // tm, nh),
        in_specs=[pl.BlockSpec((1, tm, d), lambda b, i, h: (b, i, 0)),
                  pl.BlockSpec((1, d), lambda b, i, h: (0, 0)),
                  pl.BlockSpec((1, 1, d), lambda b, i, h: (b, 0, 0)),
                  pl.BlockSpec((1, 1, d), lambda b, i, h: (b, 0, 0)),
                  pl.BlockSpec((qd, d), lambda b, i, h: (h, 0)),
                  pl.BlockSpec((1, nk, PEER_HALF), lambda b, i, h: (h, 0, 0)),
                  pl.BlockSpec((1, nk, PEER_HALF), lambda b, i, h: (h, 0, 0))],
        out_specs=[pl.BlockSpec((1, d, tm), lambda b, i, h: (b, 0, i)),
                   pl.BlockSpec((1, 1, tm), lambda b, i, h: (b, 0, i)),
                   tab_spec, tab_spec, tab_spec, tab_spec],
        out_shape=[jax.ShapeDtypeStruct((bsz, d, l), FP8), jax.ShapeDtypeStruct((bsz, 1, l), F32),
                   tab, tab, tab16, tab16],
        scratch_shapes=[pltpu.VMEM((d, tm), BF16), pltpu.VMEM((_CAND_ROWS, tm), F32)],
        compiler_params=_cparams(("parallel", "parallel", "arbitrary")),
        name="peer_route",
    )(x, g, sc, sh, wq_t, k1, k2)


def _gelu_tanh(x):
    return 0.5 * x * (1.0 + jnp.tanh(0.7978845608028654 * (x + 0.044715 * (x * x * x))))


EXPERT_ROUNDS = 8


def _expert_kernel(hq_ref, st_ref, uq_ref, su_ref, vq_ref, sv_ref, cnt_ref, a_ref, r2_ref, b_ref,
                   o_ref, act_scr, w_scr, p16_scr, p8_scr, sp_scr, *, te, tm):
    j = pl.program_id(2)
    nt = pl.num_programs(2) - 2
    nk = PEER_N_KEYS
    cur = j % 2
    prv = 1 - cur
    jt = jnp.minimum(j, nt - 1)
    d = vq_ref.shape[1]
    nlc = tm // LANES

    @pl.when(j == 0)
    def _():
        o_ref[...] = jnp.zeros_like(o_ref)
        act_scr[...] = jnp.zeros_like(act_scr)
        w_scr[...] = jnp.zeros_like(w_scr)
        p8_scr[...] = jnp.zeros_like(p8_scr)
        sp_scr[...] = jnp.zeros_like(sp_scr)

    nr = EXPERT_ROUNDS
    rd, re = d // nr, te // nr
    mh, nh = te // 2, tm // 2
    assert nk % re == 0 and nr == 8
    pmax = [jnp.zeros((BF16_ROWS, LANES), BF16) for _ in range(nlc)]
    for r in range(nr):
        rows = slice(r * rd, (r + 1) * rd)
        o_ref[0, rows, :] += _dot(vq_ref[0, rows, :], p8_scr[cur]) * sp_scr[cur]
        a_idx = jt * (te // nk) + (r * re) // nk
        k0 = (r * re) % nk
        for tc in range(tm // LANES):
            lanes = slice(tc * LANES, (tc + 1) * LANES)
            cnt8 = cnt_ref[0, a_idx, :, lanes]
            fa8 = a_ref[0, a_idx, :, lanes]
            w = [jnp.zeros((BF16_ROWS, LANES), BF16) for _ in range(re // BF16_ROWS)]
            for hh in range(PEER_HEADS):
                cnt = jnp.broadcast_to(cnt8[hh:hh + 1], (BF16_ROWS, LANES)).astype(BF16)
                fa = jnp.broadcast_to(fa8[hh:hh + 1], (BF16_ROWS, LANES)).astype(BF16)
                for sub in range(re // BF16_ROWS):
                    ks = slice(k0 + sub * BF16_ROWS, k0 + (sub + 1) * BF16_ROWS)
                    sel = r2_ref[0, hh, ks, lanes] < cnt
                    w[sub] = w[sub] + jnp.where(sel, b_ref[0, hh, ks, lanes], 0) * fa
            for sub in range(re // BF16_ROWS):
                e0 = r * re + sub * BF16_ROWS
                w_scr[cur, e0:e0 + BF16_ROWS, lanes] = w[sub]
        er = slice(r * re, (r + 1) * re)
        for lc in range(nlc):
            lanes = slice(lc * LANES, (lc + 1) * LANES)
            g = _gelu_tanh(act_scr[prv, er, lanes].astype(BF16))
            pp = w_scr[prv, er, lanes] * (g * sv_ref[er, :])
            p16_scr[er, lanes] = pp
            pa = jnp.abs(pp)
            for sub in range(re // BF16_ROWS):
                pmax[lc] = jnp.maximum(pmax[lc], pa[sub * BF16_ROWS:(sub + 1) * BF16_ROWS])
        if r % 2 == 0:
            qm, qn = (r // 2) // 2, (r // 2) % 2
            er = slice(qm * mh, (qm + 1) * mh)
            acc = _dot(uq_ref[er, :], hq_ref[0, :, qn * nh:(qn + 1) * nh])
            for lc in range(nh // LANES):
                lanes = slice(qn * nh + lc * LANES, qn * nh + (lc + 1) * LANES)
                act_scr[cur, er, lanes] = (acc[:, lc * LANES:(lc + 1) * LANES] * su_ref[er, :]
                                           * st_ref[0, :, lanes])

    for lc in range(nlc):
        lanes = slice(lc * LANES, (lc + 1) * LANES)
        amax = jnp.max(pmax[lc].astype(F32), axis=0, keepdims=True)
        inv = (FP8_MAX / jnp.maximum(amax, FP8_TINY)).astype(BF16)
        sp_scr[prv, :, lanes] = 1.0 / inv.astype(F32)
        p8_scr[prv, :, lanes] = (p16_scr[:, lanes] * inv).astype(FP8)


PEER_TE = 1024


def _peer_experts(hq, st, uq, su, vq, sv, cnt, fa, r2, fb, tm=512):
    bsz, d, l = hq.shape
    ne = uq.shape[0]
    nh, nk = PEER_HEADS, PEER_N_KEYS
    te = PEER_TE
    nt = ne // te
    tab_spec = pl.BlockSpec((1, nh, nk, tm), lambda b, i, j: (b, 0, 0, i),
                            pipeline_mode=pl.Buffered(1))
    tab_a_spec = pl.BlockSpec((1, nk, nh, tm), lambda b, i, j: (b, 0, 0, i),
                              pipeline_mode=pl.Buffered(1))
    return pl.pallas_call(
        functools.partial(_expert_kernel, te=te, tm=tm),
        grid=(bsz, l // tm, nt + 2),
        in_specs=[pl.BlockSpec((1, d, tm), lambda b, i, j: (b, 0, i), pipeline_mode=pl.Buffered(1)),
                  pl.BlockSpec((1, 1, tm), lambda b, i, j: (b, 0, i)),
                  pl.BlockSpec((te, d), lambda b, i, j: (jnp.minimum(j, nt - 1), 0)),
                  pl.BlockSpec((te, LANES), lambda b, i, j: (jnp.minimum(j, nt - 1), 0)),
                  pl.BlockSpec((1, d, te), lambda b, i, j: (jnp.maximum(j - 2, 0), 0, 0)),
                  pl.BlockSpec((te, LANES), lambda b, i, j: (jnp.clip(j - 1, 0, nt - 1), 0)),
                  tab_a_spec, tab_a_spec, tab_spec, tab_spec],
        out_specs=pl.BlockSpec((1, d, tm), lambda b, i, j: (b, 0, i)),
        out_shape=jax.ShapeDtypeStruct((bsz, d, l), F32),
        scratch_shapes=[pltpu.VMEM((2, te, tm), F32), pltpu.VMEM((2, te, tm), BF16),
                        pltpu.VMEM((te, tm), BF16), pltpu.VMEM((2, te, tm), FP8),
                        pltpu.VMEM((2, 1, tm), F32)],
        compiler_params=_cparams(("parallel", "parallel", "arbitrary")),
        name="peer_experts",
    )(hq, st, uq, su, vq, sv, cnt, fa, r2, fb)


def _final_kernel(x_ref, pt_ref, g2_ref, fg_ref, o_ref):
    x = x_ref[0] + g2_ref[0] * pt_ref[0].T
    ms = jnp.mean(x * x, axis=-1, keepdims=True)
    o_ref[0] = (x * lax.rsqrt(ms + EPS)) * fg_ref[...]


def _final(x, peer_t, g2, fg, tm=256):
    bsz, l, d = x.shape
    return pl.pallas_call(
        _final_kernel,
        grid=(bsz, l // tm),
        in_specs=[pl.BlockSpec((1, tm, d), lambda b, i: (b, i, 0)),
                  pl.BlockSpec((1, d, tm), lambda b, i: (b, 0, i)),
                  pl.BlockSpec((1, 1, d), lambda b, i: (b, 0, 0)),
                  pl.BlockSpec((1, d), lambda b, i: (0, 0))],
        out_specs=pl.BlockSpec((1, tm, d), lambda b, i: (b, i, 0)),
        out_shape=jax.ShapeDtypeStruct((bsz, l, d), F32),
        compiler_params=_cparams(("parallel", "parallel")),
        name="final_norm",
    )(x, peer_t, g2, fg)


def _quant_rows_kernel(w_ref, q_ref, s_ref, *, transpose):
    w = w_ref[...]
    amax = jnp.max(jnp.abs(w), axis=1, keepdims=True)
    s = (jnp.maximum(amax, FP8_TINY) * (1.0 / FP8_MAX)).astype(s_ref.dtype)
    q = w / s.astype(F32)
    s_ref[...] = jnp.broadcast_to(s, s_ref.shape)
    if transpose:
        q_ref[0] = q.T.astype(FP8)
    else:
        q_ref[...] = q.astype(FP8)


def _quant_rows(w, transpose, scale_dtype, rows=512):
    e, d = w.shape
    per = PEER_TE // rows
    if transpose:
        q_shape = jax.ShapeDtypeStruct((e // PEER_TE, d, PEER_TE), FP8)
        q_spec = pl.BlockSpec((1, d, rows), lambda i: (i // per, 0, i % per))
    else:
        q_shape = jax.ShapeDtypeStruct((e, d), FP8)
        q_spec = pl.BlockSpec((rows, d), lambda i: (i, 0))
    return pl.pallas_call(
        functools.partial(_quant_rows_kernel, transpose=transpose),
        grid=(e // rows,),
        in_specs=[pl.BlockSpec((rows, d), lambda i: (i, 0))],
        out_specs=[q_spec, pl.BlockSpec((rows, LANES), lambda i: (i, 0))],
        out_shape=[q_shape, jax.ShapeDtypeStruct((e, LANES), scale_dtype)],
        compiler_params=_cparams(("parallel",)),
        name="quant_rows_t" if transpose else "quant_rows",
    )(w)


def _prepare_weights(w_in, gate_b, na_rpb, w_out, peer_wq, peer_k1, peer_k2, peer_u, peer_v):
    mw = MLSTM_HEADS * MLSTM_HEAD_DIM
    n_gates = 4 * MLSTM_HEADS
    w_main = jnp.concatenate([w_in[:, :4 * mw], w_in[:, 4 * mw + n_gates:]], axis=1).astype(BF16)
    w_gate = jnp.pad(w_in[:, 4 * mw:4 * mw + n_gates], ((0, 0), (0, LANES - n_gates))).astype(BF16)
    uq, su = _quant_rows(peer_u, False, F32)
    vq, sv = _quant_rows(peer_v, True, BF16)
    return dict(
        w_main=w_main, w_gate=w_gate,
        gate_b=jnp.pad(gate_b, (0, LANES - n_gates))[None, :],
        na_bias=_na_bias_table(na_rpb),
        w_out=w_out.astype(BF16),
        wq_t=peer_wq.T.astype(BF16),
        k1=peer_k1, k2=peer_k2,
        uq=uq, su=su, vq=vq, sv=sv)


def _trunk(x, mod, norm1_g, mlstm_norm_g, norm2_g, final_g, wts):
    bsz, l, d = x.shape
    sh1, sc1, g1, sh2, sc2, g2 = [m[:, None, :] for m in jnp.split(mod, 6, axis=-1)]
    mw = MLSTM_HEADS * MLSTM_HEAD_DIM
    proj, gates = _inproj(x, norm1_g, sc1, sh1, wts["w_main"], wts["w_gate"])
    t = min(256, l)
    gates = _gate_prep(gates, wts["gate_b"], t)
    g4 = gates[:, :, :4 * MLSTM_HEADS].reshape(bsz, l, 4, MLSTM_HEADS)
    gc = g4.transpose(0, 3, 1, 2)
    gr = g4.transpose(0, 3, 2, 1)
    h_bwd = _mlstm_direction(proj, gc, gr, True, t=t)
    hm = _mlstm_direction(proj, gc, gr, False, h_bwd=h_bwd, norm_g=mlstm_norm_g, t=t)
    hn = _neighbourhood_attention(proj, wts["na_bias"], 4 * mw // NA_HEAD_DIM)
    x1 = _outproj(hm, hn, wts["w_out"], x, g1)
    hq, st, cnt, fa, r2, fb = _peer_route(x1, norm2_g, sc2, sh2, wts["wq_t"], wts["k1"], wts["k2"])
    peer_t = _peer_experts(hq, st, wts["uq"], wts["su"], wts["vq"], wts["sv"],
                           cnt.transpose(0, 2, 1, 3), fa.transpose(0, 2, 1, 3), r2, fb)
    return _final(x1, peer_t, g2, final_g)


def kernel(x_prompt, x_sample, c_prompt, c_sample, ada_w, ada_b, norm1_g, w_in, gate_b, mlstm_norm_g, na_rpb, w_out, norm2_g, peer_wq, peer_k1, peer_k2, peer_u, peer_v, final_g):
    assert ada_w.shape[0] == 1, "single-layer trunk"
    nb_p = c_prompt.shape[0]
    nb_s = c_sample.shape[0]
    rows = -(-(nb_p + nb_s) // 8) * 8
    c_all = jnp.concatenate([c_prompt, c_sample], axis=0)
    c_all = jnp.pad(c_all, ((0, rows - nb_p - nb_s), (0, 0)))
    mod = _modulation(c_all, ada_w[0], ada_b)
    wts = _prepare_weights(w_in[0], gate_b[0], na_rpb[0], w_out[0], peer_wq[0], peer_k1[0],
                           peer_k2[0], peer_u[0], peer_v[0])
    fg = final_g[None, :]
    y_prompt = _trunk(x_prompt, mod[:nb_p], norm1_g, mlstm_norm_g, norm2_g, fg, wts)
    y_sample = _trunk(x_sample, mod[nb_p:nb_p + nb_s], norm1_g, mlstm_norm_g, norm2_g, fg, wts)
    return (y_prompt, y_sample)
```

```python
import functools

import jax
import jax.numpy as jnp
from jax import lax
from jax.experimental import pallas as pl
from jax.experimental.pallas import tpu as pltpu

F32 = jnp.float32
BF16 = jnp.bfloat16
FP8 = jnp.float8_e4m3fn
FP8_MAX = 448.0
FP8_TINY = 1e-30

GRID_W = 64
MLSTM_HEADS = 8
MLSTM_HEAD_DIM = 256
NA_HEADS = 16
NA_HEAD_DIM = 128
NA_WIN_ROWS = 8
NA_WIN_COLS = 16
PEER_HEADS = 8
PEER_HALF = 128
PEER_N_KEYS = 128
PEER_TOPK = 16
EPS = 1e-6

LANES = 128
BF16_ROWS = 16
VMEM_LIMIT = 56 * 1024 * 1024

NEG_INF = float("-inf")


def _cparams(sem):
    return pltpu.CompilerParams(dimension_semantics=sem, vmem_limit_bytes=VMEM_LIMIT)


def _dot(a, b):
    return jnp.dot(a, b, preferred_element_type=F32)


def _dot_nt(a, b):
    return lax.dot_general(a, b, (((1,), (1,)), ((), ())), preferred_element_type=F32)


def _mod_kernel(c_ref, w_ref, b_ref, o_ref):
    c = c_ref[...]
    s = c * jax.nn.sigmoid(c)
    hi = s.astype(BF16)
    lo = (s - hi.astype(F32)).astype(BF16)
    w = w_ref[...].astype(BF16)
    o_ref[...] = _dot(hi, w) + _dot(lo, w) + b_ref[...]


def _modulation(c, w, b, tn=512):
    r, d = c.shape
    n = w.shape[1]
    return pl.pallas_call(
        _mod_kernel,
        grid=(n // tn,),
        in_specs=[pl.BlockSpec((r, d), lambda j: (0, 0)),
                  pl.BlockSpec((d, tn), lambda j: (0, j)),
                  pl.BlockSpec((1, tn), lambda j: (0, j))],
        out_specs=pl.BlockSpec((r, tn), lambda j: (0, j)),
        out_shape=jax.ShapeDtypeStruct((r, n), F32),
        compiler_params=_cparams(("parallel",)),
        name="modulation",
    )(c, w, b)


def _ada_norm(x, g, sc, sh):
    ms = jnp.mean(x * x, axis=-1, keepdims=True)
    y = x * lax.rsqrt(ms + EPS)
    return (y * g) * (1.0 + sc) + sh


def _inproj_kernel(x_ref, g_ref, sc_ref, sh_ref, w_ref, wg_ref, o_ref, og_ref, h_scr):
    @pl.when(pl.program_id(2) == 0)
    def _():
        h = _ada_norm(x_ref[0], g_ref[...], sc_ref[0], sh_ref[0]).astype(BF16)
        h_scr[...] = h
        og_ref[0] = _dot(h, wg_ref[...])

    o_ref[0] = _dot(h_scr[...], w_ref[...]).astype(BF16)


def _inproj(x, g, sc, sh, w, wg, tm=512, tn=1024):
    bsz, l, d = x.shape
    n = w.shape[1]
    ng = wg.shape[1]
    return pl.pallas_call(
        _inproj_kernel,
        grid=(bsz, l // tm, n // tn),
        in_specs=[pl.BlockSpec((1, tm, d), lambda b, i, j: (b, i, 0)),
                  pl.BlockSpec((1, d), lambda b, i, j: (0, 0)),
                  pl.BlockSpec((1, 1, d), lambda b, i, j: (b, 0, 0)),
                  pl.BlockSpec((1, 1, d), lambda b, i, j: (b, 0, 0)),
                  pl.BlockSpec((d, tn), lambda b, i, j: (0, j)),
                  pl.BlockSpec((d, ng), lambda b, i, j: (0, 0))],
        out_specs=[pl.BlockSpec((1, tm, tn), lambda b, i, j: (b, i, j)),
                   pl.BlockSpec((1, tm, ng), lambda b, i, j: (b, i, 0))],
        out_shape=[jax.ShapeDtypeStruct((bsz, l, n), BF16),
                   jax.ShapeDtypeStruct((bsz, l, ng), F32)],
        scratch_shapes=[pltpu.VMEM((tm, d), BF16)],
        compiler_params=_cparams(("parallel", "parallel", "arbitrary")),
        name="inproj",
    )(x, g, sc, sh, w, wg)


def _log_sigmoid(x):
    return jnp.minimum(x, 0.0) - jnp.log(1.0 + jnp.exp(-jnp.abs(x)))


def _split3(x):
    hi = x.astype(BF16)
    r = x - hi.astype(F32)
    mid = r.astype(BF16)
    lo = (r - mid.astype(F32)).astype(BF16)
    return hi, mid, lo


def _gate_prep_kernel(g_ref, b_ref, o_ref):
    g = g_ref[0] + b_ref[...]
    t = g.shape[0]
    nh = MLSTM_HEADS
    r_i = lax.broadcasted_iota(jnp.int32, (t, t), 0)
    c_i = lax.broadcasted_iota(jnp.int32, (t, t), 1)
    lower = jnp.where(c_i <= r_i, 1.0, 0.0).astype(BF16)
    upper = jnp.where(c_i >= r_i, 1.0, 0.0).astype(BF16)
    pieces = _split3(_log_sigmoid(g))
    pre = sum(_dot(lower, p) for p in pieces)
    suf = sum(_dot(upper, p) for p in pieces)
    lane = lax.broadcasted_iota(jnp.int32, g.shape, 1)
    out = jnp.where((lane >= nh) & (lane < 2 * nh), pre, g)
    o_ref[0] = jnp.where((lane >= 3 * nh) & (lane < 4 * nh), suf, out)


def _gate_prep(gates, bias, t):
    bsz, l, n = gates.shape
    return pl.pallas_call(
        _gate_prep_kernel,
        grid=(bsz, l // t),
        in_specs=[pl.BlockSpec((1, t, n), lambda b, c: (b, c, 0)),
                  pl.BlockSpec((1, n), lambda b, c: (0, 0))],
        out_specs=pl.BlockSpec((1, t, n), lambda b, c: (b, c, 0)),
        out_shape=jax.ShapeDtypeStruct((bsz, l, n), F32),
        compiler_params=_cparams(("parallel", "parallel")),
        name="gate_prep",
    )(gates, bias)


def _mlstm_chunk(q, k, v, i_col, b_col, i_row, b_row, c_st, n_st, m_st, reverse):
    t = q.shape[0]
    kscale = MLSTM_HEAD_DIM ** -0.5
    r_i = lax.broadcasted_iota(jnp.int32, (t, t), 0)
    c_i = lax.broadcasted_iota(jnp.int32, (t, t), 1)
    seen = (c_i >= r_i) if reverse else (c_i <= r_i)
    d = jnp.where(seen, b_col - b_row + i_row, NEG_INF)
    inter = b_col + m_st
    m_t = jnp.maximum(inter, jnp.max(d, axis=1, keepdims=True))
    w_inter = jnp.exp(inter - m_t)
    p = jnp.exp(d - m_t) * (_dot_nt(q, k) * kscale)
    num = w_inter * _dot(q, c_st.astype(BF16)) + _dot(p.astype(BF16), v)
    den = (w_inter * jnp.sum(q.astype(F32) * n_st, axis=1, keepdims=True)
           + jnp.sum(p, axis=1, keepdims=True))
    h = num / jnp.maximum(jnp.abs(den), jnp.exp(-m_t))
    b_last = b_col[0:1] if reverse else b_col[t - 1:t]
    dec = b_last - b_col + i_col
    m_new = jnp.maximum(b_last + m_st, jnp.max(dec, axis=0, keepdims=True))
    w_old = jnp.exp(b_last + m_st - m_new)
    w_s = jnp.exp(dec - m_new)
    kw = k.astype(F32) * (w_s * kscale)
    c_new = w_old * c_st + _dot(kw.T.astype(BF16), v)
    n_new = w_old * n_st + jnp.sum(kw, axis=0, keepdims=True)
    return h, c_new, n_new, m_new


def _mlstm_kernel(*refs, reverse, hb):
    if reverse:
        q_ref, k_ref, v_ref, gc_ref, gr_ref, o_ref, c_scr, n_scr, m_scr = refs
    else:
        (q_ref, k_ref, v_ref, gc_ref, gr_ref, hb_ref, mo_ref, ng_ref,
         o_ref, c_scr, n_scr, m_scr) = refs

    @pl.when(pl.program_id(2) == 0)
    def _():
        c_scr[...] = jnp.zeros_like(c_scr)
        n_scr[...] = jnp.zeros_like(n_scr)
        m_scr[...] = jnp.zeros_like(m_scr)

    dh = MLSTM_HEAD_DIM
    gi = 2 if reverse else 0
    outs, cs, ns, ms = [], [], [], []
    for hh in range(hb):
        cols = slice(hh * dh, (hh + 1) * dh)
        gc = gc_ref[0, hh]
        gr = gr_ref[0, hh]
        h, c_new, n_new, m_new = _mlstm_chunk(
            q_ref[0, :, cols], k_ref[0, :, cols], v_ref[0, :, cols],
            gc[:, gi:gi + 1], gc[:, gi + 1:gi + 2], gr[gi:gi + 1, :], gr[gi + 1:gi + 2, :],
            c_scr[hh], n_scr[hh], m_scr[hh], reverse)
        cs.append(c_new)
        ns.append(n_new)
        ms.append(m_new)
        if reverse:
            outs.append(h)
        else:
            ht = h + hb_ref[0, :, cols]
            ht = ht * lax.rsqrt(jnp.mean(ht * ht, axis=-1, keepdims=True) + EPS)
            ht = ht * ng_ref[:, cols]
            outs.append((ht * jax.nn.sigmoid(mo_ref[0, :, cols].astype(F32))).astype(BF16))
    o_ref[0] = jnp.concatenate(outs, axis=1)
    c_scr[...] = jnp.stack(cs)
    n_scr[...] = jnp.stack(ns)
    m_scr[...] = jnp.stack(ms)


def _mlstm_direction(pm, gc, gr, reverse, h_bwd=None, norm_g=None, t=256, hb=2):
    bsz, l, _ = pm.shape
    nh = MLSTM_HEADS
    dh = MLSTM_HEAD_DIM
    w = nh * dh
    nc = l // t
    ng = nh // hb
    wb = hb * dh

    def ci(c):
        return (nc - 1 - c) if reverse else c

    in_specs = [pl.BlockSpec((1, t, wb), lambda b, g, c: (b, ci(c), g)),
                pl.BlockSpec((1, t, wb), lambda b, g, c: (b, ci(c), ng + g)),
                pl.BlockSpec((1, t, wb), lambda b, g, c: (b, ci(c), 2 * ng + g)),
                pl.BlockSpec((1, hb, t, 4), lambda b, g, c: (b, g, ci(c), 0)),
                pl.BlockSpec((1, hb, 4, t), lambda b, g, c: (b, g, 0, ci(c)))]
    args = [pm, pm, pm, gc, gr]
    if not reverse:
        in_specs += [pl.BlockSpec((1, t, wb), lambda b, g, c: (b, c, g)),
                     pl.BlockSpec((1, t, wb), lambda b, g, c: (b, c, 3 * ng + g)),
                     pl.BlockSpec((1, wb), lambda b, g, c: (0, g))]
        args += [h_bwd, pm, norm_g]
    return pl.pallas_call(
        functools.partial(_mlstm_kernel, reverse=reverse, hb=hb),
        grid=(bsz, ng, nc),
        in_specs=in_specs,
        out_specs=pl.BlockSpec((1, t, wb), lambda b, g, c: (b, ci(c), g)),
        out_shape=jax.ShapeDtypeStruct((bsz, l, w), F32 if reverse else BF16),
        scratch_shapes=[pltpu.VMEM((hb, dh, dh), F32),
                        pltpu.VMEM((hb, 1, dh), F32),
                        pltpu.VMEM((hb, 1, 1), F32)],
        compiler_params=_cparams(("parallel", "parallel", "arbitrary")),
        name="mlstm_bwd" if reverse else "mlstm_fwd",
    )(*args)


def _na_bias_table(rpb):
    c = jnp.arange(GRID_W)
    cs = jnp.clip(c - NA_WIN_COLS // 2, 0, GRID_W - NA_WIN_COLS)
    col_ok = (c[None, :] >= cs[:, None]) & (c[None, :] < cs[:, None] + NA_WIN_COLS)
    dc = jnp.clip(c[None, :] - c[:, None], -(NA_WIN_COLS - 1), NA_WIN_COLS - 1) + (NA_WIN_COLS - 1)
    onehot = (dc[None] == jnp.arange(2 * NA_WIN_COLS - 1)[:, None, None]).astype(F32)
    t1 = jnp.einsum("hdc,cqk->hdqk", rpb.astype(F32), onehot, precision=lax.Precision.HIGHEST)
    t1 = jnp.where(col_ok[None, None], t1, NEG_INF)
    fill = jnp.full_like(t1[:, :1], NEG_INF)
    lo = jnp.concatenate([fill, t1], axis=1)
    hi = jnp.concatenate([t1, fill], axis=1)
    return jnp.concatenate([lo, hi], axis=-1)


NA_RBLK = 8
NA_KROWS = 2 * NA_WIN_ROWS


def _na_block(q_ref, kw, vw, tp_ref, o_ref, p_scr, shift, starts):
    w2 = 2 * GRID_W
    scale = NA_HEAD_DIM ** -0.5
    s = _dot_nt(q_ref[0], kw) * scale
    lane = lax.broadcasted_iota(jnp.int32, (GRID_W, w2), 1)
    inv_l = []
    for jr in range(NA_RBLK):
        rows = slice(jr * GRID_W, (jr + 1) * GRID_W)
        st = starts[jr]
        blocks = {}
        for pr in range(NA_KROWS // 2):
            ok0 = st <= 2 * pr < st + NA_WIN_ROWS
            ok1 = st <= 2 * pr + 1 < st + NA_WIN_ROWS
            if not (ok0 or ok1):
                p_scr[rows, pr * w2:(pr + 1) * w2] = jnp.zeros((GRID_W, w2), BF16)
                continue
            d0 = shift + 2 * pr - jr + (NA_WIN_ROWS - 1)
            x = s[rows, pr * w2:(pr + 1) * w2] + tp_ref[0, d0 + 1]
            if not ok0:
                x = jnp.where(lane >= GRID_W, x, NEG_INF)
            if not ok1:
                x = jnp.where(lane < GRID_W, x, NEG_INF)
            blocks[pr] = x
        m = functools.reduce(jnp.maximum, blocks.values())
        m = jnp.max(m, axis=-1, keepdims=True)
        tot = None
        for pr, x in blocks.items():
            p = jnp.exp(x - m)
            tot = p if tot is None else tot + p
            p_scr[rows, pr * w2:(pr + 1) * w2] = p.astype(BF16)
        inv_l.append(1.0 / jnp.sum(tot, axis=-1, keepdims=True))
    o = _dot(p_scr[...], vw)
    for jr in range(NA_RBLK):
        rows = slice(jr * GRID_W, (jr + 1) * GRID_W)
        o_ref[0, rows, :] = (o[rows] * inv_l[jr]).astype(BF16)


def _na_kernel(q_ref, k_ref, v_ref, tp_ref, o_ref, p_scr, *, rows):
    i = pl.program_id(2)
    nblk = rows // NA_RBLK
    half = NA_WIN_ROWS // 2
    kwin = NA_KROWS * GRID_W

    def run(base, shift, starts):
        kpos = base * GRID_W
        if not isinstance(kpos, int):
            kpos = pl.multiple_of(kpos, GRID_W)
        _na_block(q_ref, k_ref[0, pl.ds(kpos, kwin), :], v_ref[0, pl.ds(kpos, kwin), :],
                  tp_ref, o_ref, p_scr, shift, starts)

    @pl.when(i == 0)
    def _():
        run(0, 0, [max(jr - half, 0) for jr in range(NA_RBLK)])

    @pl.when(i == nblk - 1)
    def _():
        run(rows - NA_KROWS, NA_RBLK - NA_KROWS,
            [min(jr + half, NA_KROWS - NA_WIN_ROWS) for jr in range(NA_RBLK)])

    @pl.when(jnp.logical_and(i > 0, i < nblk - 1))
    def _():
        run(i * NA_RBLK - half, -half, list(range(NA_RBLK)))


def _neighbourhood_attention(pn, bias_tab, col0):
    bsz, l, _ = pn.shape
    nh, dh = NA_HEADS, NA_HEAD_DIM
    rows = l // GRID_W
    assert rows >= NA_KROWS and rows % NA_RBLK == 0
    tq = NA_RBLK * GRID_W
    return pl.pallas_call(
        functools.partial(_na_kernel, rows=rows),
        grid=(bsz, nh, rows // NA_RBLK),
        in_specs=[pl.BlockSpec((1, tq, dh), lambda b, h, i: (b, i, col0 + h)),
                  pl.BlockSpec((1, l, dh), lambda b, h, i: (b, 0, col0 + nh + h)),
                  pl.BlockSpec((1, l, dh), lambda b, h, i: (b, 0, col0 + 2 * nh + h)),
                  pl.BlockSpec((1, 2 * NA_WIN_ROWS, GRID_W, 2 * GRID_W), lambda b, h, i: (h, 0, 0, 0))],
        out_specs=pl.BlockSpec((1, tq, dh), lambda b, h, i: (b, i, h)),
        out_shape=jax.ShapeDtypeStruct((bsz, l, nh * dh), BF16),
        scratch_shapes=[pltpu.VMEM((tq, NA_KROWS * GRID_W), BF16)],
        compiler_params=_cparams(("parallel", "parallel", "arbitrary")),
        name="natten",
    )(pn, pn, pn, bias_tab)


def _outproj_kernel(hm_ref, hn_ref, w1_ref, w2_ref, x_ref, g_ref, o_ref):
    acc = _dot(hm_ref[0], w1_ref[...]) + _dot(hn_ref[0], w2_ref[...])
    o_ref[0] = x_ref[0] + g_ref[0] * acc


def _outproj(hm, hn, w, x, g, tm=512, tn=1024):
    bsz, l, d = x.shape
    kh = hm.shape[2]
    return pl.pallas_call(
        _outproj_kernel,
        grid=(bsz, l // tm, d // tn),
        in_specs=[pl.BlockSpec((1, tm, kh), lambda b, i, j: (b, i, 0)),
                  pl.BlockSpec((1, tm, kh), lambda b, i, j: (b, i, 0)),
                  pl.BlockSpec((kh, tn), lambda b, i, j: (0, j)),
                  pl.BlockSpec((kh, tn), lambda b, i, j: (1, j)),
                  pl.BlockSpec((1, tm, tn), lambda b, i, j: (b, i, j)),
                  pl.BlockSpec((1, 1, tn), lambda b, i, j: (b, 0, j))],
        out_specs=pl.BlockSpec((1, tm, tn), lambda b, i, j: (b, i, j)),
        out_shape=jax.ShapeDtypeStruct((bsz, l, d), F32),
        compiler_params=_cparams(("parallel", "parallel", "parallel")),
        name="outproj",
    )(hm, hn, w, w, x, g)


def _top_values_axis0(chunks, k, group):
    vals = [[] for _ in chunks]
    for g0 in range(0, len(chunks), group):
        ids = range(g0, min(g0 + group, len(chunks)))
        state = {c: chunks[c] for c in ids}
        for _ in range(k):
            for c in ids:
                mx = jnp.max(state[c], axis=0, keepdims=True)
                vals[c].append(mx)
                state[c] = jnp.where(state[c] == mx, NEG_INF, state[c])
    return vals


def _count_reaching(rows, x, strict):
    cnt = jnp.zeros_like(x)
    for row in rows:
        cnt = cnt + jnp.where((row > x) if strict else (row >= x), 1.0, 0.0)
    return cnt


_CAND_PAIRS = [(a, b) for a in range(PEER_TOPK + 1) for b in range(PEER_TOPK + 1)
               if (a + 1) * (b + 1) <= PEER_TOPK + 1]
_CAND_ROWS = -(-len(_CAND_PAIRS) // 8) * 8


def _split2(x):
    hi = x.astype(BF16)
    return hi, (x - hi.astype(F32)).astype(BF16)


def _dot_f32(a, b):
    a0, a1 = _split2(a)
    b0, b1 = _split2(b)
    return _dot(a0, b0) + (_dot(a0, b1) + _dot(a1, b0))


def _route_kernel(x_ref, g_ref, sc_ref, sh_ref, wq_ref, k1_ref, k2_ref,
                  hq_ref, st_ref, cnt_ref, a_ref, r2_ref, b_ref, ht_scr, cand_scr):
    @pl.when(pl.program_id(2) == 0)
    def _():
        h = _ada_norm(x_ref[0], g_ref[...], sc_ref[0], sh_ref[0])
        ht = h.T
        ht_scr[...] = ht.astype(BF16)
        st = jnp.maximum(jnp.max(jnp.abs(ht), axis=0, keepdims=True), FP8_TINY) * (1.0 / FP8_MAX)
        st_ref[0] = st
        hq_ref[0] = (ht / st).astype(FP8)

    qt = _dot(wq_ref[...], ht_scr[...])
    s1 = _dot_f32(k1_ref[0], qt[:PEER_HALF])
    s2 = _dot_f32(k2_ref[0], qt[PEER_HALF:])
    nchunk = s1.shape[1] // LANES
    lanes = [slice(c * LANES, (c + 1) * LANES) for c in range(nchunk)]
    s1c = [s1[:, ln] for ln in lanes]
    s2c = [s2[:, ln] for ln in lanes]
    ktop = PEER_TOPK + 1
    v1 = _top_values_axis0(s1c, ktop, group=2)
    v2 = _top_values_axis0(s2c, ktop, group=2)
    cand_scr[...] = jnp.full(cand_scr.shape, NEG_INF, F32)
    for c in range(nchunk):
        for i, (ka, kb) in enumerate(_CAND_PAIRS):
            cand_scr[i:i + 1, lanes[c]] = v1[c][ka] + v2[c][kb]
    sc = _top_values_axis0([cand_scr[:, ln] for ln in lanes], ktop, group=nchunk)
    for c in range(nchunk):
        z = jnp.ones_like(sc[c][0])
        for kk in range(1, PEER_TOPK):
            z = z + jnp.exp(sc[c][kk] - sc[c][0])
        thr = 0.5 * (sc[c][PEER_TOPK - 1] + sc[c][PEER_TOPK]) - s1c[c]
        cnt_ref[0, 0, :, lanes[c]] = _count_reaching(v2[c], thr, strict=False)
        r2_ref[0, 0, :, lanes[c]] = _count_reaching(v2[c], s2c[c], strict=True).astype(BF16)
        a_ref[0, 0, :, lanes[c]] = jnp.exp(s1c[c] - v1[c][0]) / z
        b_ref[0, 0, :, lanes[c]] = jnp.exp(s2c[c] - v2[c][0]).astype(BF16)


def _peer_route(x, g, sc, sh, wq_t, k1, k2, tm=512):
    bsz, l, d = x.shape
    nh = PEER_HEADS
    qd = 2 * PEER_HALF
    nk = PEER_N_KEYS
    tab = jax.ShapeDtypeStruct((bsz, nh, nk, l), F32)
    tab16 = jax.ShapeDtypeStruct((bsz, nh, nk, l), BF16)
    tab_spec = pl.BlockSpec((1, 1, nk, tm), lambda b, i, h: (b, h, 0, i))
    return pl.pallas_call(
        _route_kernel,
        grid=(bsz, l // tm, nh),
        in_specs=[pl.BlockSpec((1, tm, d), lambda b, i, h: (b, i, 0)),
                  pl.BlockSpec((1, d), lambda b, i, h: (0, 0)),
                  pl.BlockSpec((1, 1, d), lambda b, i, h: (b, 0, 0)),
                  pl.BlockSpec((1, 1, d), lambda b, i, h: (b, 0, 0)),
                  pl.BlockSpec((qd, d), lambda b, i, h: (h, 0)),
                  pl.BlockSpec((1, nk, PEER_HALF), lambda b, i, h: (h, 0, 0)),
                  pl.BlockSpec((1, nk, PEER_HALF), lambda b, i, h: (h, 0, 0))],
        out_specs=[pl.BlockSpec((1, d, tm), lambda b, i, h: (b, 0, i)),
                   pl.BlockSpec((1, 1, tm), lambda b, i, h: (b, 0, i)),
                   tab_spec, tab_spec, tab_spec, tab_spec],
        out_shape=[jax.ShapeDtypeStruct((bsz, d, l), FP8), jax.ShapeDtypeStruct((bsz, 1, l), F32),
                   tab, tab, tab16, tab16],
        scratch_shapes=[pltpu.VMEM((d, tm), BF16), pltpu.VMEM((_CAND_ROWS, tm), F32)],
        compiler_params=_cparams(("parallel", "parallel", "arbitrary")),
        name="peer_route",
    )(x, g, sc, sh, wq_t, k1, k2)


def _gelu_tanh(x):
    return 0.5 * x * (1.0 + jnp.tanh(0.7978845608028654 * (x + 0.044715 * (x * x * x))))


EXPERT_ROUNDS = 8


def _expert_kernel(hq_ref, st_ref, uq_ref, su_ref, vq_ref, sv_ref, cnt_ref, a_ref, r2_ref, b_ref,
                   o_ref, act_scr, w_scr, p16_scr, p8_scr, sp_scr, *, te, tm):
    j = pl.program_id(2)
    nt = pl.num_programs(2) - 2
    nk = PEER_N_KEYS
    cur = j % 2
    prv = 1 - cur
    jt = jnp.minimum(j, nt - 1)
    d = vq_ref.shape[1]
    nlc = tm // LANES

    @pl.when(j == 0)
    def _():
        o_ref[...] = jnp.zeros_like(o_ref)
        act_scr[...] = jnp.zeros_like(act_scr)
        w_scr[...] = jnp.zeros_like(w_scr)
        p8_scr[...] = jnp.zeros_like(p8_scr)
        sp_scr[...] = jnp.zeros_like(sp_scr)

    nr = EXPERT_ROUNDS
    rd, re = d // nr, te // nr
    mh, nh = te // 2, tm // 2
    assert nk % re == 0 and nr == 8
    def step(new_tile):
        pmax = [jnp.zeros((BF16_ROWS, LANES), BF16) for _ in range(nlc)]
        for r in range(nr):
            rows = slice(r * rd, (r + 1) * rd)
            o_ref[0, rows, :] += _dot(vq_ref[0, rows, :], p8_scr[cur]) * sp_scr[cur]
            a_idx = jt * (te // nk) + (r * re) // nk
            k0 = (r * re) % nk
            for tc in range(nlc if new_tile else 0):
                lanes = slice(tc * LANES, (tc + 1) * LANES)
                cnt8 = cnt_ref[0, a_idx, :, lanes]
                fa8 = a_ref[0, a_idx, :, lanes]
                w = [jnp.zeros((BF16_ROWS, LANES), BF16) for _ in range(re // BF16_ROWS)]
                for hh in range(PEER_HEADS):
                    cnt = jnp.broadcast_to(cnt8[hh:hh + 1], (BF16_ROWS, LANES)).astype(BF16)
                    fa = jnp.broadcast_to(fa8[hh:hh + 1], (BF16_ROWS, LANES)).astype(BF16)
                    for sub in range(re // BF16_ROWS):
                        ks = slice(k0 + sub * BF16_ROWS, k0 + (sub + 1) * BF16_ROWS)
                        sel = r2_ref[0, hh, ks, lanes] < cnt
                        w[sub] = w[sub] + jnp.where(sel, b_ref[0, hh, ks, lanes], 0) * fa
                for sub in range(re // BF16_ROWS):
                    e0 = r * re + sub * BF16_ROWS
                    w_scr[cur, e0:e0 + BF16_ROWS, lanes] = w[sub]
            er = slice(r * re, (r + 1) * re)
            for lc in range(nlc):
                lanes = slice(lc * LANES, (lc + 1) * LANES)
                g = _gelu_tanh(act_scr[prv, er, lanes].astype(BF16))
                pp = w_scr[prv, er, lanes] * (g * sv_ref[er, :])
                p16_scr[er, lanes] = pp
                pa = jnp.abs(pp)
                for sub in range(re // BF16_ROWS):
                    pmax[lc] = jnp.maximum(pmax[lc], pa[sub * BF16_ROWS:(sub + 1) * BF16_ROWS])
            if new_tile and r % 2 == 0:
                qm, qn = (r // 2) // 2, (r // 2) % 2
                er = slice(qm * mh, (qm + 1) * mh)
                acc = _dot(uq_ref[er, :], hq_ref[0, :, qn * nh:(qn + 1) * nh])
                for lc in range(nh // LANES):
                    lanes = slice(qn * nh + lc * LANES, qn * nh + (lc + 1) * LANES)
                    act_scr[cur, er, lanes] = (acc[:, lc * LANES:(lc + 1) * LANES] * su_ref[er, :]
                                               * st_ref[0, :, lanes])

        for lc in range(nlc):
            lanes = slice(lc * LANES, (lc + 1) * LANES)
            amax = jnp.max(pmax[lc].astype(F32), axis=0, keepdims=True)
            inv = (FP8_MAX / jnp.maximum(amax, FP8_TINY)).astype(BF16)
            sp_scr[prv, :, lanes] = 1.0 / inv.astype(F32)
            p8_scr[prv, :, lanes] = (p16_scr[:, lanes] * inv).astype(FP8)

    pl.when(j < nt)(functools.partial(step, True))
    pl.when(j >= nt)(functools.partial(step, False))


PEER_TE = 1024


def _peer_experts(hq, st, uq, su, vq, sv, cnt, fa, r2, fb, tm=512):
    bsz, d, l = hq.shape
    ne = uq.shape[0]
    nh, nk = PEER_HEADS, PEER_N_KEYS
    te = PEER_TE
    nt = ne // te
    tab_spec = pl.BlockSpec((1, nh, nk, tm), lambda b, i, j: (b, 0, 0, i),
                            pipeline_mode=pl.Buffered(1))
    tab_a_spec = pl.BlockSpec((1, nk, nh, tm), lambda b, i, j: (b, 0, 0, i),
                              pipeline_mode=pl.Buffered(1))
    return pl.pallas_call(
        functools.partial(_expert_kernel, te=te, tm=tm),
        grid=(bsz, l // tm, nt + 2),
        in_specs=[pl.BlockSpec((1, d, tm), lambda b, i, j: (b, 0, i), pipeline_mode=pl.Buffered(1)),
                  pl.BlockSpec((1, 1, tm), lambda b, i, j: (b, 0, i)),
                  pl.BlockSpec((te, d), lambda b, i, j: (jnp.minimum(j, nt - 1), 0)),
                  pl.BlockSpec((te, LANES), lambda b, i, j: (jnp.minimum(j, nt - 1), 0)),
                  pl.BlockSpec((1, d, te), lambda b, i, j: (jnp.maximum(j - 2, 0), 0, 0)),
                  pl.BlockSpec((te, LANES), lambda b, i, j: (jnp.clip(j - 1, 0, nt - 1), 0)),
                  tab_a_spec, tab_a_spec, tab_spec, tab_spec],
        out_specs=pl.BlockSpec((1, d, tm), lambda b, i, j: (b, 0, i)),
        out_shape=jax.ShapeDtypeStruct((bsz, d, l), F32),
        scratch_shapes=[pltpu.VMEM((2, te, tm), F32), pltpu.VMEM((2, te, tm), BF16),
                        pltpu.VMEM((te, tm), BF16), pltpu.VMEM((2, te, tm), FP8),
                        pltpu.VMEM((2, 1, tm), F32)],
        compiler_params=_cparams(("parallel", "parallel", "arbitrary")),
        name="peer_experts",
    )(hq, st, uq, su, vq, sv, cnt, fa, r2, fb)


def _final_kernel(x_ref, pt_ref, g2_ref, fg_ref, o_ref):
    x = x_ref[0] + g2_ref[0] * pt_ref[0].T
    ms = jnp.mean(x * x, axis=-1, keepdims=True)
    o_ref[0] = (x * lax.rsqrt(ms + EPS)) * fg_ref[...]


def _final(x, peer_t, g2, fg, tm=256):
    bsz, l, d = x.shape
    return pl.pallas_call(
        _final_kernel,
        grid=(bsz, l // tm),
        in_specs=[pl.BlockSpec((1, tm, d), lambda b, i: (b, i, 0)),
                  pl.BlockSpec((1, d, tm), lambda b, i: (b, 0, i)),
                  pl.BlockSpec((1, 1, d), lambda b, i: (b, 0, 0)),
                  pl.BlockSpec((1, d), lambda b, i: (0, 0))],
        out_specs=pl.BlockSpec((1, tm, d), lambda b, i: (b, i, 0)),
        out_shape=jax.ShapeDtypeStruct((bsz, l, d), F32),
        compiler_params=_cparams(("parallel", "parallel")),
        name="final_norm",
    )(x, peer_t, g2, fg)


def _quant_rows_kernel(w_ref, q_ref, s_ref, *, transpose):
    w = w_ref[...]
    amax = jnp.max(jnp.abs(w), axis=1, keepdims=True)
    s = (jnp.maximum(amax, FP8_TINY) * (1.0 / FP8_MAX)).astype(s_ref.dtype)
    q = w / s.astype(F32)
    s_ref[...] = jnp.broadcast_to(s, s_ref.shape)
    if transpose:
        q_ref[0] = q.T.astype(FP8)
    else:
        q_ref[...] = q.astype(FP8)


def _quant_rows(w, transpose, scale_dtype, rows=512):
    e, d = w.shape
    per = PEER_TE // rows
    if transpose:
        q_shape = jax.ShapeDtypeStruct((e // PEER_TE, d, PEER_TE), FP8)
        q_spec = pl.BlockSpec((1, d, rows), lambda i: (i // per, 0, i % per))
    else:
        q_shape = jax.ShapeDtypeStruct((e, d), FP8)
        q_spec = pl.BlockSpec((rows, d), lambda i: (i, 0))
    return pl.pallas_call(
        functools.partial(_quant_rows_kernel, transpose=transpose),
        grid=(e // rows,),
        in_specs=[pl.BlockSpec((rows, d), lambda i: (i, 0))],
        out_specs=[q_spec, pl.BlockSpec((rows, LANES), lambda i: (i, 0))],
        out_shape=[q_shape, jax.ShapeDtypeStruct((e, LANES), scale_dtype)],
        compiler_params=_cparams(("parallel",)),
        name="quant_rows_t" if transpose else "quant_rows",
    )(w)


def _prepare_weights(w_in, gate_b, na_rpb, w_out, peer_wq, peer_k1, peer_k2, peer_u, peer_v):
    mw = MLSTM_HEADS * MLSTM_HEAD_DIM
    n_gates = 4 * MLSTM_HEADS
    w_main = jnp.concatenate([w_in[:, :4 * mw], w_in[:, 4 * mw + n_gates:]], axis=1).astype(BF16)
    w_gate = jnp.pad(w_in[:, 4 * mw:4 * mw + n_gates], ((0, 0), (0, LANES - n_gates))).astype(BF16)
    uq, su = _quant_rows(peer_u, False, F32)
    vq, sv = _quant_rows(peer_v, True, BF16)
    return dict(
        w_main=w_main, w_gate=w_gate,
        gate_b=jnp.pad(gate_b, (0, LANES - n_gates))[None, :],
        na_bias=_na_bias_table(na_rpb),
        w_out=w_out.astype(BF16),
        wq_t=peer_wq.T.astype(BF16),
        k1=peer_k1, k2=peer_k2,
        uq=uq, su=su, vq=vq, sv=sv)


def _trunk(x, mod, norm1_g, mlstm_norm_g, norm2_g, final_g, wts):
    bsz, l, d = x.shape
    sh1, sc1, g1, sh2, sc2, g2 = [m[:, None, :] for m in jnp.split(mod, 6, axis=-1)]
    mw = MLSTM_HEADS * MLSTM_HEAD_DIM
    proj, gates = _inproj(x, norm1_g, sc1, sh1, wts["w_main"], wts["w_gate"])
    t = min(256, l)
    gates = _gate_prep(gates, wts["gate_b"], t)
    g4 = gates[:, :, :4 * MLSTM_HEADS].reshape(bsz, l, 4, MLSTM_HEADS)
    gc = g4.transpose(0, 3, 1, 2)
    gr = g4.transpose(0, 3, 2, 1)
    h_bwd = _mlstm_direction(proj, gc, gr, True, t=t)
    hm = _mlstm_direction(proj, gc, gr, False, h_bwd=h_bwd, norm_g=mlstm_norm_g, t=t)
    hn = _neighbourhood_attention(proj, wts["na_bias"], 4 * mw // NA_HEAD_DIM)
    x1 = _outproj(hm, hn, wts["w_out"], x, g1)
    hq, st, cnt, fa, r2, fb = _peer_route(x1, norm2_g, sc2, sh2, wts["wq_t"], wts["k1"], wts["k2"])
    peer_t = _peer_experts(hq, st, wts["uq"], wts["su"], wts["vq"], wts["sv"],
                           cnt.transpose(0, 2, 1, 3), fa.transpose(0, 2, 1, 3), r2, fb)
    return _final(x1, peer_t, g2, final_g)


def kernel(x_prompt, x_sample, c_prompt, c_sample, ada_w, ada_b, norm1_g, w_in, gate_b, mlstm_norm_g, na_rpb, w_out, norm2_g, peer_wq, peer_k1, peer_k2, peer_u, peer_v, final_g):
    assert ada_w.shape[0] == 1, "single-layer trunk"
    nb_p = c_prompt.shape[0]
    nb_s = c_sample.shape[0]
    rows = -(-(nb_p + nb_s) // 8) * 8
    c_all = jnp.concatenate([c_prompt, c_sample], axis=0)
    c_all = jnp.pad(c_all, ((0, rows - nb_p - nb_s), (0, 0)))
    mod = _modulation(c_all, ada_w[0], ada_b)
    wts = _prepare_weights(w_in[0], gate_b[0], na_rpb[0], w_out[0], peer_wq[0], peer_k1[0],
                           peer_k2[0], peer_u[0], peer_v[0])
    fg = final_g[None, :]
    y_prompt = _trunk(x_prompt, mod[:nb_p], norm1_g, mlstm_norm_g, norm2_g, fg, wts)
    y_sample = _trunk(x_sample, mod[nb_p:nb_p + nb_s], norm1_g, mlstm_norm_g, norm2_g, fg, wts)
    return (y_prompt, y_sample)
```

```python
import functools

import jax
import jax.numpy as jnp
from jax import lax
from jax.experimental import pallas as pl
from jax.experimental.pallas import tpu as pltpu

F32 = jnp.float32
BF16 = jnp.bfloat16
FP8 = jnp.float8_e4m3fn
FP8_MAX = 448.0
FP8_TINY = 1e-30

GRID_W = 64
MLSTM_HEADS = 8
MLSTM_HEAD_DIM = 256
NA_HEADS = 16
NA_HEAD_DIM = 128
NA_WIN_ROWS = 8
NA_WIN_COLS = 16
PEER_HEADS = 8
PEER_HALF = 128
PEER_N_KEYS = 128
PEER_TOPK = 16
EPS = 1e-6

LANES = 128
BF16_ROWS = 16
VMEM_LIMIT = 56 * 1024 * 1024

NEG_INF = float("-inf")


def _cparams(sem):
    return pltpu.CompilerParams(dimension_semantics=sem, vmem_limit_bytes=VMEM_LIMIT)


def _dot(a, b):
    return jnp.dot(a, b, preferred_element_type=F32)


def _dot_nt(a, b):
    return lax.dot_general(a, b, (((1,), (1,)), ((), ())), preferred_element_type=F32)


def _mod_kernel(c_ref, w_ref, b_ref, o_ref):
    c = c_ref[...]
    s = c * jax.nn.sigmoid(c)
    hi = s.astype(BF16)
    lo = (s - hi.astype(F32)).astype(BF16)
    w = w_ref[...].astype(BF16)
    o_ref[...] = _dot(hi, w) + _dot(lo, w) + b_ref[...]


def _modulation(c, w, b, tn=512):
    r, d = c.shape
    n = w.shape[1]
    return pl.pallas_call(
        _mod_kernel,
        grid=(n // tn,),
        in_specs=[pl.BlockSpec((r, d), lambda j: (0, 0)),
                  pl.BlockSpec((d, tn), lambda j: (0, j)),
                  pl.BlockSpec((1, tn), lambda j: (0, j))],
        out_specs=pl.BlockSpec((r, tn), lambda j: (0, j)),
        out_shape=jax.ShapeDtypeStruct((r, n), F32),
        compiler_params=_cparams(("parallel",)),
        name="modulation",
    )(c, w, b)


def _ada_norm(x, g, sc, sh):
    ms = jnp.mean(x * x, axis=-1, keepdims=True)
    y = x * lax.rsqrt(ms + EPS)
    return (y * g) * (1.0 + sc) + sh


def _inproj_kernel(x_ref, g_ref, sc_ref, sh_ref, w_ref, wg_ref, o_ref, og_ref, h_scr):
    @pl.when(pl.program_id(2) == 0)
    def _():
        h = _ada_norm(x_ref[0], g_ref[...], sc_ref[0], sh_ref[0]).astype(BF16)
        h_scr[...] = h
        og_ref[0] = _dot(h, wg_ref[...])

    o_ref[0] = _dot(h_scr[...], w_ref[...]).astype(BF16)


def _inproj(x, g, sc, sh, w, wg, tm=512, tn=1024):
    bsz, l, d = x.shape
    n = w.shape[1]
    ng = wg.shape[1]
    return pl.pallas_call(
        _inproj_kernel,
        grid=(bsz, l // tm, n // tn),
        in_specs=[pl.BlockSpec((1, tm, d), lambda b, i, j: (b, i, 0)),
                  pl.BlockSpec((1, d), lambda b, i, j: (0, 0)),
                  pl.BlockSpec((1, 1, d), lambda b, i, j: (b, 0, 0)),
                  pl.BlockSpec((1, 1, d), lambda b, i, j: (b, 0, 0)),
                  pl.BlockSpec((d, tn), lambda b, i, j: (0, j)),
                  pl.BlockSpec((d, ng), lambda b, i, j: (0, 0))],
        out_specs=[pl.BlockSpec((1, tm, tn), lambda b, i, j: (b, i, j)),
                   pl.BlockSpec((1, tm, ng), lambda b, i, j: (b, i, 0))],
        out_shape=[jax.ShapeDtypeStruct((bsz, l, n), BF16),
                   jax.ShapeDtypeStruct((bsz, l, ng), F32)],
        scratch_shapes=[pltpu.VMEM((tm, d), BF16)],
        compiler_params=_cparams(("parallel", "parallel", "arbitrary")),
        name="inproj",
    )(x, g, sc, sh, w, wg)


def _log_sigmoid(x):
    return jnp.minimum(x, 0.0) - jnp.log(1.0 + jnp.exp(-jnp.abs(x)))


def _split3(x):
    hi = x.astype(BF16)
    r = x - hi.astype(F32)
    mid = r.astype(BF16)
    lo = (r - mid.astype(F32)).astype(BF16)
    return hi, mid, lo


def _gate_prep_kernel(g_ref, b_ref, o_ref):
    g = g_ref[0] + b_ref[...]
    t = g.shape[0]
    nh = MLSTM_HEADS
    r_i = lax.broadcasted_iota(jnp.int32, (t, t), 0)
    c_i = lax.broadcasted_iota(jnp.int32, (t, t), 1)
    lower = jnp.where(c_i <= r_i, 1.0, 0.0).astype(BF16)
    upper = jnp.where(c_i >= r_i, 1.0, 0.0).astype(BF16)
    pieces = _split3(_log_sigmoid(g))
    pre = sum(_dot(lower, p) for p in pieces)
    suf = sum(_dot(upper, p) for p in pieces)
    lane = lax.broadcasted_iota(jnp.int32, g.shape, 1)
    out = jnp.where((lane >= nh) & (lane < 2 * nh), pre, g)
    o_ref[0] = jnp.where((lane >= 3 * nh) & (lane < 4 * nh), suf, out)


def _gate_prep(gates, bias, t):
    bsz, l, n = gates.shape
    return pl.pallas_call(
        _gate_prep_kernel,
        grid=(bsz, l // t),
        in_specs=[pl.BlockSpec((1, t, n), lambda b, c: (b, c, 0)),
                  pl.BlockSpec((1, n), lambda b, c: (0, 0))],
        out_specs=pl.BlockSpec((1, t, n), lambda b, c: (b, c, 0)),
        out_shape=jax.ShapeDtypeStruct((bsz, l, n), F32),
        compiler_params=_cparams(("parallel", "parallel")),
        name="gate_prep",
    )(gates, bias)


def _mlstm_chunk(q, k, v, i_col, b_col, i_row, b_row, c_st, n_st, m_st, reverse):
    t = q.shape[0]
    kscale = MLSTM_HEAD_DIM ** -0.5
    r_i = lax.broadcasted_iota(jnp.int32, (t, t), 0)
    c_i = lax.broadcasted_iota(jnp.int32, (t, t), 1)
    seen = (c_i >= r_i) if reverse else (c_i <= r_i)
    d = jnp.where(seen, b_col - b_row + i_row, NEG_INF)
    inter = b_col + m_st
    m_t = jnp.maximum(inter, jnp.max(d, axis=1, keepdims=True))
    w_inter = jnp.exp(inter - m_t)
    p = jnp.exp(d - m_t) * (_dot_nt(q, k) * kscale)
    num = w_inter * _dot(q, c_st.astype(BF16)) + _dot(p.astype(BF16), v)
    den = (w_inter * jnp.sum(q.astype(F32) * n_st, axis=1, keepdims=True)
           + jnp.sum(p, axis=1, keepdims=True))
    h = num / jnp.maximum(jnp.abs(den), jnp.exp(-m_t))
    b_last = b_col[0:1] if reverse else b_col[t - 1:t]
    dec = b_last - b_col + i_col
    m_new = jnp.maximum(b_last + m_st, jnp.max(dec, axis=0, keepdims=True))
    w_old = jnp.exp(b_last + m_st - m_new)
    w_s = jnp.exp(dec - m_new)
    kw = k.astype(F32) * (w_s * kscale)
    c_new = w_old * c_st + _dot(kw.T.astype(BF16), v)
    n_new = w_old * n_st + jnp.sum(kw, axis=0, keepdims=True)
    return h, c_new, n_new, m_new


def _mlstm_kernel(*refs, reverse, hb):
    if reverse:
        q_ref, k_ref, v_ref, gc_ref, gr_ref, o_ref, c_scr, n_scr, m_scr = refs
    else:
        (q_ref, k_ref, v_ref, gc_ref, gr_ref, hb_ref, mo_ref, ng_ref,
         o_ref, c_scr, n_scr, m_scr) = refs

    @pl.when(pl.program_id(2) == 0)
    def _():
        c_scr[...] = jnp.zeros_like(c_scr)
        n_scr[...] = jnp.zeros_like(n_scr)
        m_scr[...] = jnp.zeros_like(m_scr)

    dh = MLSTM_HEAD_DIM
    gi = 2 if reverse else 0
    outs, cs, ns, ms = [], [], [], []
    for hh in range(hb):
        cols = slice(hh * dh, (hh + 1) * dh)
        gc = gc_ref[0, hh]
        gr = gr_ref[0, hh]
        h, c_new, n_new, m_new = _mlstm_chunk(
            q_ref[0, :, cols], k_ref[0, :, cols], v_ref[0, :, cols],
            gc[:, gi:gi + 1], gc[:, gi + 1:gi + 2], gr[gi:gi + 1, :], gr[gi + 1:gi + 2, :],
            c_scr[hh], n_scr[hh], m_scr[hh], reverse)
        cs.append(c_new)
        ns.append(n_new)
        ms.append(m_new)
        if reverse:
            outs.append(h)
        else:
            ht = h + hb_ref[0, :, cols]
            ht = ht * lax.rsqrt(jnp.mean(ht * ht, axis=-1, keepdims=True) + EPS)
            ht = ht * ng_ref[:, cols]
            outs.append((ht * jax.nn.sigmoid(mo_ref[0, :, cols].astype(F32))).astype(BF16))
    o_ref[0] = jnp.concatenate(outs, axis=1)
    c_scr[...] = jnp.stack(cs)
    n_scr[...] = jnp.stack(ns)
    m_scr[...] = jnp.stack(ms)


def _mlstm_direction(pm, gc, gr, reverse, h_bwd=None, norm_g=None, t=256, hb=2):
    bsz, l, _ = pm.shape
    nh = MLSTM_HEADS
    dh = MLSTM_HEAD_DIM
    w = nh * dh
    nc = l // t
    ng = nh // hb
    wb = hb * dh

    def ci(c):
        return (nc - 1 - c) if reverse else c

    in_specs = [pl.BlockSpec((1, t, wb), lambda b, g, c: (b, ci(c), g)),
                pl.BlockSpec((1, t, wb), lambda b, g, c: (b, ci(c), ng + g)),
                pl.BlockSpec((1, t, wb), lambda b, g, c: (b, ci(c), 2 * ng + g)),
                pl.BlockSpec((1, hb, t, 4), lambda b, g, c: (b, g, ci(c), 0)),
                pl.BlockSpec((1, hb, 4, t), lambda b, g, c: (b, g, 0, ci(c)))]
    args = [pm, pm, pm, gc, gr]
    if not reverse:
        in_specs += [pl.BlockSpec((1, t, wb), lambda b, g, c: (b, c, g)),
                     pl.BlockSpec((1, t, wb), lambda b, g, c: (b, c, 3 * ng + g)),
                     pl.BlockSpec((1, wb), lambda b, g, c: (0, g))]
        args += [h_bwd, pm, norm_g]
    return pl.pallas_call(
        functools.partial(_mlstm_kernel, reverse=reverse, hb=hb),
        grid=(bsz, ng, nc),
        in_specs=in_specs,
        out_specs=pl.BlockSpec((1, t, wb), lambda b, g, c: (b, ci(c), g)),
        out_shape=jax.ShapeDtypeStruct((bsz, l, w), F32 if reverse else BF16),
        scratch_shapes=[pltpu.VMEM((hb, dh, dh), F32),
                        pltpu.VMEM((hb, 1, dh), F32),
                        pltpu.VMEM((hb, 1, 1), F32)],
        compiler_params=_cparams(("parallel", "parallel", "arbitrary")),
        name="mlstm_bwd" if reverse else "mlstm_fwd",
    )(*args)


def _na_bias_table(rpb):
    c = jnp.arange(GRID_W)
    cs = jnp.clip(c - NA_WIN_COLS // 2, 0, GRID_W - NA_WIN_COLS)
    col_ok = (c[None, :] >= cs[:, None]) & (c[None, :] < cs[:, None] + NA_WIN_COLS)
    dc = jnp.clip(c[None, :] - c[:, None], -(NA_WIN_COLS - 1), NA_WIN_COLS - 1) + (NA_WIN_COLS - 1)
    onehot = (dc[None] == jnp.arange(2 * NA_WIN_COLS - 1)[:, None, None]).astype(F32)
    t1 = jnp.einsum("hdc,cqk->hdqk", rpb.astype(F32), onehot, precision=lax.Precision.HIGHEST)
    t1 = jnp.where(col_ok[None, None], t1, NEG_INF)
    fill = jnp.full_like(t1[:, :1], NEG_INF)
    lo = jnp.concatenate([fill, t1], axis=1)
    hi = jnp.concatenate([t1, fill], axis=1)
    return jnp.concatenate([lo, hi], axis=-1)


NA_RBLK = 8
NA_KROWS = 2 * NA_WIN_ROWS


def _na_block(q_ref, kw, vw, tp_ref, o_ref, p_scr, shift, starts):
    w2 = 2 * GRID_W
    scale = NA_HEAD_DIM ** -0.5
    s = _dot_nt(q_ref[0], kw) * scale
    lane = lax.broadcasted_iota(jnp.int32, (GRID_W, w2), 1)
    inv_l = []
    for jr in range(NA_RBLK):
        rows = slice(jr * GRID_W, (jr + 1) * GRID_W)
        st = starts[jr]
        blocks = {}
        for pr in range(NA_KROWS // 2):
            ok0 = st <= 2 * pr < st + NA_WIN_ROWS
            ok1 = st <= 2 * pr + 1 < st + NA_WIN_ROWS
            if not (ok0 or ok1):
                p_scr[rows, pr * w2:(pr + 1) * w2] = jnp.zeros((GRID_W, w2), BF16)
                continue
            d0 = shift + 2 * pr - jr + (NA_WIN_ROWS - 1)
            x = s[rows, pr * w2:(pr + 1) * w2] + tp_ref[0, d0 + 1]
            if not ok0:
                x = jnp.where(lane >= GRID_W, x, NEG_INF)
            if not ok1:
                x = jnp.where(lane < GRID_W, x, NEG_INF)
            blocks[pr] = x
        m = functools.reduce(jnp.maximum, blocks.values())
        m = jnp.max(m, axis=-1, keepdims=True)
        tot = None
        for pr, x in blocks.items():
            p = jnp.exp(x - m)
            tot = p if tot is None else tot + p
            p_scr[rows, pr * w2:(pr + 1) * w2] = p.astype(BF16)
        inv_l.append(1.0 / jnp.sum(tot, axis=-1, keepdims=True))
    o = _dot(p_scr[...], vw)
    for jr in range(NA_RBLK):
        rows = slice(jr * GRID_W, (jr + 1) * GRID_W)
        o_ref[0, rows, :] = (o[rows] * inv_l[jr]).astype(BF16)


def _na_kernel(q_ref, k_ref, v_ref, tp_ref, o_ref, p_scr, *, rows):
    i = pl.program_id(2)
    nblk = rows // NA_RBLK
    half = NA_WIN_ROWS // 2
    kwin = NA_KROWS * GRID_W

    def run(base, shift, starts):
        kpos = base * GRID_W
        if not isinstance(kpos, int):
            kpos = pl.multiple_of(kpos, GRID_W)
        _na_block(q_ref, k_ref[0, pl.ds(kpos, kwin), :], v_ref[0, pl.ds(kpos, kwin), :],
                  tp_ref, o_ref, p_scr, shift, starts)

    @pl.when(i == 0)
    def _():
        run(0, 0, [max(jr - half, 0) for jr in range(NA_RBLK)])

    @pl.when(i == nblk - 1)
    def _():
        run(rows - NA_KROWS, NA_RBLK - NA_KROWS,
            [min(jr + half, NA_KROWS - NA_WIN_ROWS) for jr in range(NA_RBLK)])

    @pl.when(jnp.logical_and(i > 0, i < nblk - 1))
    def _():
        run(i * NA_RBLK - half, -half, list(range(NA_RBLK)))


def _neighbourhood_attention(pn, bias_tab, col0):
    bsz, l, _ = pn.shape
    nh, dh = NA_HEADS, NA_HEAD_DIM
    rows = l // GRID_W
    assert rows >= NA_KROWS and rows % NA_RBLK == 0
    tq = NA_RBLK * GRID_W
    return pl.pallas_call(
        functools.partial(_na_kernel, rows=rows),
        grid=(bsz, nh, rows // NA_RBLK),
        in_specs=[pl.BlockSpec((1, tq, dh), lambda b, h, i: (b, i, col0 + h)),
                  pl.BlockSpec((1, l, dh), lambda b, h, i: (b, 0, col0 + nh + h)),
                  pl.BlockSpec((1, l, dh), lambda b, h, i: (b, 0, col0 + 2 * nh + h)),
                  pl.BlockSpec((1, 2 * NA_WIN_ROWS, GRID_W, 2 * GRID_W), lambda b, h, i: (h, 0, 0, 0))],
        out_specs=pl.BlockSpec((1, tq, dh), lambda b, h, i: (b, i, h)),
        out_shape=jax.ShapeDtypeStruct((bsz, l, nh * dh), BF16),
        scratch_shapes=[pltpu.VMEM((tq, NA_KROWS * GRID_W), BF16)],
        compiler_params=_cparams(("parallel", "parallel", "arbitrary")),
        name="natten",
    )(pn, pn, pn, bias_tab)


def _outproj_kernel(hm_ref, hn_ref, w1_ref, w2_ref, x_ref, g_ref, o_ref):
    acc = _dot(hm_ref[0], w1_ref[...]) + _dot(hn_ref[0], w2_ref[...])
    o_ref[0] = x_ref[0] + g_ref[0] * acc


def _outproj(hm, hn, w, x, g, tm=512, tn=1024):
    bsz, l, d = x.shape
    kh = hm.shape[2]
    return pl.pallas_call(
        _outproj_kernel,
        grid=(bsz, l // tm, d // tn),
        in_specs=[pl.BlockSpec((1, tm, kh), lambda b, i, j: (b, i, 0)),
                  pl.BlockSpec((1, tm, kh), lambda b, i, j: (b, i, 0)),
                  pl.BlockSpec((kh, tn), lambda b, i, j: (0, j)),
                  pl.BlockSpec((kh, tn), lambda b, i, j: (1, j)),
                  pl.BlockSpec((1, tm, tn), lambda b, i, j: (b, i, j)),
                  pl.BlockSpec((1, 1, tn), lambda b, i, j: (b, 0, j))],
        out_specs=pl.BlockSpec((1, tm, tn), lambda b, i, j: (b, i, j)),
        out_shape=jax.ShapeDtypeStruct((bsz, l, d), F32),
        compiler_params=_cparams(("parallel", "parallel", "parallel")),
        name="outproj",
    )(hm, hn, w, w, x, g)


def _top_values_axis0(chunks, k, group, with_rank=False):
    vals = [[] for _ in chunks]
    ranks = [None] * len(chunks)
    for g0 in range(0, len(chunks), group):
        ids = range(g0, min(g0 + group, len(chunks)))
        state = {c: chunks[c] for c in ids}
        rank = {c: jnp.full(chunks[c].shape, float(k), F32) for c in ids} if with_rank else None
        for i in range(k):
            for c in ids:
                mx = jnp.max(state[c], axis=0, keepdims=True)
                vals[c].append(mx)
                hit = state[c] == mx
                if with_rank:
                    rank[c] = jnp.where(hit, float(i), rank[c])
                state[c] = jnp.where(hit, NEG_INF, state[c])
        if with_rank:
            for c in ids:
                ranks[c] = rank[c]
    return (vals, ranks) if with_rank else vals


def _count_reaching(rows, x):
    cnt = jnp.zeros_like(x)
    for row in rows:
        cnt = cnt + jnp.where(row >= x, 1.0, 0.0)
    return cnt


_CAND_PAIRS = [(a, b) for a in range(PEER_TOPK + 1) for b in range(PEER_TOPK + 1)
               if (a + 1) * (b + 1) <= PEER_TOPK + 1]
_CAND_ROWS = -(-len(_CAND_PAIRS) // 8) * 8


def _split2(x):
    hi = x.astype(BF16)
    return hi, (x - hi.astype(F32)).astype(BF16)


def _dot_f32(a, b):
    a0, a1 = _split2(a)
    b0, b1 = _split2(b)
    return _dot(a0, b0) + (_dot(a0, b1) + _dot(a1, b0))


def _route_kernel(x_ref, g_ref, sc_ref, sh_ref, wq0_ref, wqn_ref, k1_ref, k2_ref,
                  hq_ref, st_ref, cnt_ref, a_ref, r2_ref, b_ref, ht_scr, qt_scr, cand_scr):
    hd = pl.program_id(2)
    cur = hd % 2
    nxt = 1 - cur

    @pl.when(hd == 0)
    def _():
        h = _ada_norm(x_ref[0], g_ref[...], sc_ref[0], sh_ref[0])
        ht = h.T
        ht_scr[...] = ht.astype(BF16)
        st = jnp.maximum(jnp.max(jnp.abs(ht), axis=0, keepdims=True), FP8_TINY) * (1.0 / FP8_MAX)
        st_ref[0] = st
        hq_ref[0] = (ht / st).astype(FP8)
        qt_scr[0] = _dot(wq0_ref[...], ht_scr[...])

    qt = qt_scr[cur]
    s1 = _dot_f32(k1_ref[0], qt[:PEER_HALF])
    s2 = _dot_f32(k2_ref[0], qt[PEER_HALF:])
    nchunk = s1.shape[1] // LANES
    lanes = [slice(c * LANES, (c + 1) * LANES) for c in range(nchunk)]
    s1c = [s1[:, ln] for ln in lanes]
    s2c = [s2[:, ln] for ln in lanes]
    ktop = PEER_TOPK + 1
    qt_scr[nxt, :PEER_HALF] = _dot(wqn_ref[:PEER_HALF], ht_scr[...])
    v1 = _top_values_axis0(s1c, ktop, group=2)
    qt_scr[nxt, PEER_HALF:] = _dot(wqn_ref[PEER_HALF:], ht_scr[...])
    v2, r2 = _top_values_axis0(s2c, ktop, group=2, with_rank=True)
    cand_scr[...] = jnp.full(cand_scr.shape, NEG_INF, F32)
    for c in range(nchunk):
        for i, (ka, kb) in enumerate(_CAND_PAIRS):
            cand_scr[i:i + 1, lanes[c]] = v1[c][ka] + v2[c][kb]
    sc = _top_values_axis0([cand_scr[:, ln] for ln in lanes], ktop, group=nchunk)
    for c in range(nchunk):
        z = jnp.ones_like(sc[c][0])
        for kk in range(1, PEER_TOPK):
            z = z + jnp.exp(sc[c][kk] - sc[c][0])
        thr = 0.5 * (sc[c][PEER_TOPK - 1] + sc[c][PEER_TOPK]) - s1c[c]
        cnt_ref[0, 0, :, lanes[c]] = _count_reaching(v2[c][:PEER_TOPK], thr)
        r2_ref[0, 0, :, lanes[c]] = r2[c].astype(BF16)
        a_ref[0, 0, :, lanes[c]] = jnp.exp(s1c[c] - v1[c][0]) / z
        b_ref[0, 0, :, lanes[c]] = jnp.exp(s2c[c] - v2[c][0]).astype(BF16)


def _peer_route(x, g, sc, sh, wq_t, k1, k2, tm=512):
    bsz, l, d = x.shape
    nh = PEER_HEADS
    qd = 2 * PEER_HALF
    nk = PEER_N_KEYS
    tab = jax.ShapeDtypeStruct((bsz, nh, nk, l), F32)
    tab16 = jax.ShapeDtypeStruct((bsz, nh, nk, l), BF16)
    tab_spec = pl.BlockSpec((1, 1, nk, tm), lambda b, i, h: (b, h, 0, i))
    return pl.pallas_call(
        _route_kernel,
        grid=(bsz, l // tm, nh),
        in_specs=[pl.BlockSpec((1, tm, d), lambda b, i, h: (b, i, 0)),
                  pl.BlockSpec((1, d), lambda b, i, h: (0, 0)),
                  pl.BlockSpec((1, 1, d), lambda b, i, h: (b, 0, 0)),
                  pl.BlockSpec((1, 1, d), lambda b, i, h: (b, 0, 0)),
                  pl.BlockSpec((qd, d), lambda b, i, h: (0, 0)),
                  pl.BlockSpec((qd, d), lambda b, i, h: (jnp.minimum(h + 1, nh - 1), 0)),
                  pl.BlockSpec((1, nk, PEER_HALF), lambda b, i, h: (h, 0, 0)),
                  pl.BlockSpec((1, nk, PEER_HALF), lambda b, i, h: (h, 0, 0))],
        out_specs=[pl.BlockSpec((1, d, tm), lambda b, i, h: (b, 0, i)),
                   pl.BlockSpec((1, 1, tm), lambda b, i, h: (b, 0, i)),
                   tab_spec, tab_spec, tab_spec, tab_spec],
        out_shape=[jax.ShapeDtypeStruct((bsz, d, l), FP8), jax.ShapeDtypeStruct((bsz, 1, l), F32),
                   tab, tab, tab16, tab16],
        scratch_shapes=[pltpu.VMEM((d, tm), BF16), pltpu.VMEM((2, qd, tm), F32),
                        pltpu.VMEM((_CAND_ROWS, tm), F32)],
        compiler_params=_cparams(("parallel", "parallel", "arbitrary")),
        name="peer_route",
    )(x, g, sc, sh, wq_t, wq_t, k1, k2)


def _gelu_tanh(x):
    return 0.5 * x * (1.0 + jnp.tanh(0.7978845608028654 * (x + 0.044715 * (x * x * x))))


EXPERT_ROUNDS = 8


def _expert_kernel(hq_ref, st_ref, uq_ref, su_ref, vq_ref, sv_ref, cnt_ref, a_ref, r2_ref, b_ref,
                   o_ref, act_scr, w_scr, p16_scr, p8_scr, sp_scr, *, te, tm):
    j = pl.program_id(2)
    nt = pl.num_programs(2) - 2
    nk = PEER_N_KEYS
    cur = j % 2
    prv = 1 - cur
    jt = jnp.minimum(j, nt - 1)
    d = vq_ref.shape[1]
    nlc = tm // LANES

    @pl.when(j == 0)
    def _():
        o_ref[...] = jnp.zeros_like(o_ref)
        act_scr[...] = jnp.zeros_like(act_scr)
        w_scr[...] = jnp.zeros_like(w_scr)
        p8_scr[...] = jnp.zeros_like(p8_scr)
        sp_scr[...] = jnp.zeros_like(sp_scr)

    nr = EXPERT_ROUNDS
    rd, re = d // nr, te // nr
    mh, nh = te // 2, tm // 2
    qstride = nr // 4
    assert nk % re == 0 and nr % 4 == 0

    def step(new_tile):
        pmax = [jnp.zeros((BF16_ROWS, LANES), BF16) for _ in range(nlc)]
        for r in range(nr):
            rows = slice(r * rd, (r + 1) * rd)
            o_ref[0, rows, :] += _dot(vq_ref[0, rows, :], p8_scr[cur]) * sp_scr[cur]
            a_idx = jt * (te // nk) + (r * re) // nk
            k0 = (r * re) % nk
            for tc in range(nlc if new_tile else 0):
                lanes = slice(tc * LANES, (tc + 1) * LANES)
                cnt8 = cnt_ref[0, a_idx, :, lanes]
                fa8 = a_ref[0, a_idx, :, lanes]
                w = [jnp.zeros((BF16_ROWS, LANES), BF16) for _ in range(re // BF16_ROWS)]
                for hh in range(PEER_HEADS):
                    cnt = jnp.broadcast_to(cnt8[hh:hh + 1], (BF16_ROWS, LANES)).astype(BF16)
                    fa = jnp.broadcast_to(fa8[hh:hh + 1], (BF16_ROWS, LANES)).astype(BF16)
                    for sub in range(re // BF16_ROWS):
                        ks = slice(k0 + sub * BF16_ROWS, k0 + (sub + 1) * BF16_ROWS)
                        sel = r2_ref[0, hh, ks, lanes] < cnt
                        w[sub] = w[sub] + jnp.where(sel, b_ref[0, hh, ks, lanes], 0) * fa
                for sub in range(re // BF16_ROWS):
                    e0 = r * re + sub * BF16_ROWS
                    w_scr[cur, e0:e0 + BF16_ROWS, lanes] = w[sub]
            er = slice(r * re, (r + 1) * re)
            for lc in range(nlc):
                lanes = slice(lc * LANES, (lc + 1) * LANES)
                g = _gelu_tanh(act_scr[prv, er, lanes].astype(BF16))
                pp = w_scr[prv, er, lanes] * (g * sv_ref[er, :])
                p16_scr[er, lanes] = pp
                pa = jnp.abs(pp)
                for sub in range(re // BF16_ROWS):
                    pmax[lc] = jnp.maximum(pmax[lc], pa[sub * BF16_ROWS:(sub + 1) * BF16_ROWS])
            if new_tile and r % qstride == 0:
                qm, qn = (r // qstride) // 2, (r // qstride) % 2
                er = slice(qm * mh, (qm + 1) * mh)
                acc = _dot(uq_ref[er, :], hq_ref[0, :, qn * nh:(qn + 1) * nh])
                for lc in range(nh // LANES):
                    lanes = slice(qn * nh + lc * LANES, qn * nh + (lc + 1) * LANES)
                    act_scr[cur, er, lanes] = (acc[:, lc * LANES:(lc + 1) * LANES] * su_ref[er, :]
                                               * st_ref[0, :, lanes])

        for lc in range(nlc):
            lanes = slice(lc * LANES, (lc + 1) * LANES)
            amax = jnp.max(pmax[lc].astype(F32), axis=0, keepdims=True)
            inv = (FP8_MAX / jnp.maximum(amax, FP8_TINY)).astype(BF16)
            sp_scr[prv, :, lanes] = 1.0 / inv.astype(F32)
            p8_scr[prv, :, lanes] = (p16_scr[:, lanes] * inv).astype(FP8)

    pl.when(j < nt)(functools.partial(step, True))
    pl.when(j >= nt)(functools.partial(step, False))


PEER_TE = 1024


def _peer_experts(hq, st, uq, su, vq, sv, cnt, fa, r2, fb, tm=512):
    bsz, d, l = hq.shape
    ne = uq.shape[0]
    nh, nk = PEER_HEADS, PEER_N_KEYS
    te = PEER_TE
    nt = ne // te
    tab_spec = pl.BlockSpec((1, nh, nk, tm), lambda b, i, j: (b, 0, 0, i),
                            pipeline_mode=pl.Buffered(1))
    tab_a_spec = pl.BlockSpec((1, nk, nh, tm), lambda b, i, j: (b, 0, 0, i),
                              pipeline_mode=pl.Buffered(1))
    return pl.pallas_call(
        functools.partial(_expert_kernel, te=te, tm=tm),
        grid=(bsz, l // tm, nt + 2),
        in_specs=[pl.BlockSpec((1, d, tm), lambda b, i, j: (b, 0, i), pipeline_mode=pl.Buffered(1)),
                  pl.BlockSpec((1, 1, tm), lambda b, i, j: (b, 0, i)),
                  pl.BlockSpec((te, d), lambda b, i, j: (jnp.minimum(j, nt - 1), 0)),
                  pl.BlockSpec((te, LANES), lambda b, i, j: (jnp.minimum(j, nt - 1), 0)),
                  pl.BlockSpec((1, d, te), lambda b, i, j: (jnp.maximum(j - 2, 0), 0, 0)),
                  pl.BlockSpec((te, LANES), lambda b, i, j: (jnp.clip(j - 1, 0, nt - 1), 0)),
                  tab_a_spec, tab_a_spec, tab_spec, tab_spec],
        out_specs=pl.BlockSpec((1, d, tm), lambda b, i, j: (b, 0, i)),
        out_shape=jax.ShapeDtypeStruct((bsz, d, l), F32),
        scratch_shapes=[pltpu.VMEM((2, te, tm), F32), pltpu.VMEM((2, te, tm), BF16),
                        pltpu.VMEM((te, tm), BF16), pltpu.VMEM((2, te, tm), FP8),
                        pltpu.VMEM((2, 1, tm), F32)],
        compiler_params=_cparams(("parallel", "parallel", "arbitrary")),
        name="peer_experts",
    )(hq, st, uq, su, vq, sv, cnt, fa, r2, fb)


def _final_kernel(x_ref, pt_ref, g2_ref, fg_ref, o_ref):
    x = x_ref[0] + g2_ref[0] * pt_ref[0].T
    ms = jnp.mean(x * x, axis=-1, keepdims=True)
    o_ref[0] = (x * lax.rsqrt(ms + EPS)) * fg_ref[...]


def _final(x, peer_t, g2, fg, tm=256):
    bsz, l, d = x.shape
    return pl.pallas_call(
        _final_kernel,
        grid=(bsz, l // tm),
        in_specs=[pl.BlockSpec((1, tm, d), lambda b, i: (b, i, 0)),
                  pl.BlockSpec((1, d, tm), lambda b, i: (b, 0, i)),
                  pl.BlockSpec((1, 1, d), lambda b, i: (b, 0, 0)),
                  pl.BlockSpec((1, d), lambda b, i: (0, 0))],
        out_specs=pl.BlockSpec((1, tm, d), lambda b, i: (b, i, 0)),
        out_shape=jax.ShapeDtypeStruct((bsz, l, d), F32),
        compiler_params=_cparams(("parallel", "parallel")),
        name="final_norm",
    )(x, peer_t, g2, fg)


def _quant_rows_kernel(w_ref, q_ref, s_ref, *, transpose):
    w = w_ref[...]
    amax = jnp.max(jnp.abs(w), axis=1, keepdims=True)
    s = (jnp.maximum(amax, FP8_TINY) * (1.0 / FP8_MAX)).astype(s_ref.dtype)
    q = w / s.astype(F32)
    s_ref[...] = jnp.broadcast_to(s, s_ref.shape)
    if transpose:
        q_ref[0] = q.T.astype(FP8)
    else:
        q_ref[...] = q.astype(FP8)


def _quant_rows(w, transpose, scale_dtype, rows=512):
    e, d = w.shape
    per = PEER_TE // rows
    if transpose:
        q_shape = jax.ShapeDtypeStruct((e // PEER_TE, d, PEER_TE), FP8)
        q_spec = pl.BlockSpec((1, d, rows), lambda i: (i // per, 0, i % per))
    else:
        q_shape = jax.ShapeDtypeStruct((e, d), FP8)
        q_spec = pl.BlockSpec((rows, d), lambda i: (i, 0))
    return pl.pallas_call(
        functools.partial(_quant_rows_kernel, transpose=transpose),
        grid=(e // rows,),
        in_specs=[pl.BlockSpec((rows, d), lambda i: (i, 0))],
        out_specs=[q_spec, pl.BlockSpec((rows, LANES), lambda i: (i, 0))],
        out_shape=[q_shape, jax.ShapeDtypeStruct((e, LANES), scale_dtype)],
        compiler_params=_cparams(("parallel",)),
        name="quant_rows_t" if transpose else "quant_rows",
    )(w)


def _prepare_weights(w_in, gate_b, na_rpb, w_out, peer_wq, peer_k1, peer_k2, peer_u, peer_v):
    mw = MLSTM_HEADS * MLSTM_HEAD_DIM
    n_gates = 4 * MLSTM_HEADS
    w_main = jnp.concatenate([w_in[:, :4 * mw], w_in[:, 4 * mw + n_gates:]], axis=1).astype(BF16)
    w_gate = jnp.pad(w_in[:, 4 * mw:4 * mw + n_gates], ((0, 0), (0, LANES - n_gates))).astype(BF16)
    uq, su = _quant_rows(peer_u, False, F32)
    vq, sv = _quant_rows(peer_v, True, BF16)
    return dict(
        w_main=w_main, w_gate=w_gate,
        gate_b=jnp.pad(gate_b, (0, LANES - n_gates))[None, :],
        na_bias=_na_bias_table(na_rpb),
        w_out=w_out.astype(BF16),
        wq_t=peer_wq.T.astype(BF16),
        k1=peer_k1, k2=peer_k2,
        uq=uq, su=su, vq=vq, sv=sv)


def _trunk(x, mod, norm1_g, mlstm_norm_g, norm2_g, final_g, wts):
    bsz, l, d = x.shape
    sh1, sc1, g1, sh2, sc2, g2 = [m[:, None, :] for m in jnp.split(mod, 6, axis=-1)]
    mw = MLSTM_HEADS * MLSTM_HEAD_DIM
    proj, gates = _inproj(x, norm1_g, sc1, sh1, wts["w_main"], wts["w_gate"])
    t = min(256, l)
    gates = _gate_prep(gates, wts["gate_b"], t)
    g4 = gates[:, :, :4 * MLSTM_HEADS].reshape(bsz, l, 4, MLSTM_HEADS)
    gc = g4.transpose(0, 3, 1, 2)
    gr = g4.transpose(0, 3, 2, 1)
    h_bwd = _mlstm_direction(proj, gc, gr, True, t=t)
    hm = _mlstm_direction(proj, gc, gr, False, h_bwd=h_bwd, norm_g=mlstm_norm_g, t=t)
    hn = _neighbourhood_attention(proj, wts["na_bias"], 4 * mw // NA_HEAD_DIM)
    x1 = _outproj(hm, hn, wts["w_out"], x, g1)
    hq, st, cnt, fa, r2, fb = _peer_route(x1, norm2_g, sc2, sh2, wts["wq_t"], wts["k1"], wts["k2"])
    peer_t = _peer_experts(hq, st, wts["uq"], wts["su"], wts["vq"], wts["sv"],
                           cnt.transpose(0, 2, 1, 3), fa.transpose(0, 2, 1, 3), r2, fb)
    return _final(x1, peer_t, g2, final_g)


def kernel(x_prompt, x_sample, c_prompt, c_sample, ada_w, ada_b, norm1_g, w_in, gate_b, mlstm_norm_g, na_rpb, w_out, norm2_g, peer_wq, peer_k1, peer_k2, peer_u, peer_v, final_g):
    assert ada_w.shape[0] == 1, "single-layer trunk"
    nb_p = c_prompt.shape[0]
    nb_s = c_sample.shape[0]
    rows = -(-(nb_p + nb_s) // 8) * 8
    c_all = jnp.concatenate([c_prompt, c_sample], axis=0)
    c_all = jnp.pad(c_all, ((0, rows - nb_p - nb_s), (0, 0)))
    mod = _modulation(c_all, ada_w[0], ada_b)
    wts = _prepare_weights(w_in[0], gate_b[0], na_rpb[0], w_out[0], peer_wq[0], peer_k1[0],
                           peer_k2[0], peer_u[0], peer_v[0])
    fg = final_g[None, :]
    y_prompt = _trunk(x_prompt, mod[:nb_p], norm1_g, mlstm_norm_g, norm2_g, fg, wts)
    y_sample = _trunk(x_sample, mod[nb_p:nb_p + nb_s], norm1_g, mlstm_norm_g, norm2_g, fg, wts)
    return (y_prompt, y_sample)
```
